```python
import jax
import jax.numpy as jnp
from jax import lax
import numpy as np

D_MODEL = 1024
BATCH = 8
SEQ = 4096
DEPTH = 2

HEAD_DIM = 64
NSA_HEADS = 8
NSA_KV_HEADS = 2
MOBA_HEADS = 8
ROPE_THETA = 10000.0
CMP_BLOCK = 32
CMP_STRIDE = 16
CMP_HIDDEN = 128
SEL_BLOCK = 64
SEL_TOPN = 16
WINDOW = 512
MOBA_BLOCK = 256
MOBA_TOPK = 3
N_EXPERTS = 256
TOP_K = 8
N_GROUPS = 8
TOPK_GROUPS = 4
D_EXPERT = 256
ROUTED_SCALE = 2.5
DISPATCH_ROWS = 128
SEL_QCHUNK = 64
WIN_QBLOCK = 128
MOBA_QCHUNK = 32
DEEPNORM_ALPHA = (2 * DEPTH) ** 0.25
DEEPNORM_BETA = (8 * DEPTH) ** -0.25
LN_EPS = 1e-5
NEG = -1e30
BIG = 1e30

NSA_Q = NSA_HEADS * HEAD_DIM
NSA_KV = NSA_KV_HEADS * HEAD_DIM
MOBA_W = MOBA_HEADS * HEAD_DIM
IN_SPLITS = (NSA_Q, NSA_KV, NSA_KV, NSA_KV, NSA_KV, NSA_KV, NSA_KV, 3 * NSA_HEADS, MOBA_W, MOBA_W, MOBA_W, D_MODEL, D_MODEL)
IN_COLS = sum(IN_SPLITS)
VALUE_SPLITS = (2, 4, 6, 10)

kernel_name = 'hybrid_nsa_moba_gated_deepnorm_moe'


def rope_tables(seq):
    pos = jnp.arange(seq, dtype=jnp.float32)
    inv = ROPE_THETA ** (-jnp.arange(0, HEAD_DIM, 2, dtype=jnp.float32) / HEAD_DIM)
    ang = pos[:, None] * inv[None, :]
    ang = jnp.concatenate([ang, ang], axis=-1)
    return jnp.cos(ang), jnp.sin(ang)


def apply_rope(t, cos, sin):
    t1, t2 = jnp.split(t, 2, axis=-1)
    rot = jnp.concatenate([-t2, t1], axis=-1)
    return (t * cos + rot * sin).astype(t.dtype)


def layer_norm(x, g, b):
    xf = x.astype(jnp.float32)
    mu = jnp.mean(xf, axis=-1, keepdims=True)
    var = jnp.mean(jnp.square(xf - mu), axis=-1, keepdims=True)
    return ((xf - mu) * lax.rsqrt(var + LN_EPS) * g + b).astype(x.dtype)


def _heads(t, n_heads):
    b, s, _ = t.shape
    return t.reshape(b, s, n_heads, HEAD_DIM).transpose(0, 2, 1, 3)


def _masked_softmax(scores, mask):
    return jax.nn.softmax(jnp.where(mask, scores.astype(jnp.float32), NEG), axis=-1)


def _gather_blocks(blocks, idx):
    return jax.vmap(jax.vmap(lambda bl, ix: bl[ix]))(blocks, idx)


def _unchunk(o):
    o = jnp.moveaxis(o, 0, -3)
    return o.reshape(o.shape[0], -1, o.shape[-3] * o.shape[-2], o.shape[-1])


def nsa_compress(t, pos_emb, w1, b1, w2):
    b, g, s, d = t.shape
    n_cmp = (s - CMP_BLOCK) // CMP_STRIDE + 1
    idx = np.arange(n_cmp)[:, None] * CMP_STRIDE + np.arange(CMP_BLOCK)[None, :]
    blocks = t[:, :, idx] + pos_emb
    hid = jax.nn.gelu(blocks.reshape(b, g, n_cmp, CMP_BLOCK * d) @ w1 + b1)
    return hid @ w2


def nsa_attention(q, q_rot, k_cmp, v_cmp, k_sel, v_sel, k_win, v_win, gates, cmp_pos, cmp_w1, cmp_b1, cmp_w2):
    b, h, s, d = q.shape
    g = NSA_KV_HEADS
    hg = h // g
    scale = d ** -0.5
    pos = jnp.arange(s)
    qg = q.reshape(b, g, hg, s, d)
    qr = q_rot.reshape(b, g, hg, s, d)

    kc = nsa_compress(k_cmp, cmp_pos[0], cmp_w1[0], cmp_b1[0], cmp_w2[0])
    vc = nsa_compress(v_cmp, cmp_pos[1], cmp_w1[1], cmp_b1[1], cmp_w2[1])
    n_cmp = kc.shape[2]
    cmp_mask = (jnp.arange(n_cmp) * CMP_STRIDE + CMP_BLOCK - 1)[None, :] <= pos[:, None]
    p_cmp = _masked_softmax(jnp.einsum('bghsd,bgnd->bghsn', qg, kc) * scale, cmp_mask)
    p_cmp = p_cmp * jnp.any(cmp_mask, axis=-1, keepdims=True)
    o_cmp = jnp.einsum('bghsn,bgnd->bghsd', p_cmp.astype(vc.dtype), vc).reshape(b, h, s, d)

    n_sel = s // SEL_BLOCK
    n_top = min(SEL_TOPN, n_sel)
    c0 = np.arange(n_cmp)[:, None] * CMP_STRIDE
    s0 = np.arange(n_sel)[None, :] * SEL_BLOCK
    overlap = np.clip(np.minimum(c0 + CMP_BLOCK, s0 + SEL_BLOCK) - np.maximum(c0, s0), 0, None) / CMP_STRIDE
    imp = jnp.einsum('bghsn,nj->bgsj', p_cmp, jnp.asarray(overlap, jnp.float32))
    blk = jnp.arange(n_sel)[None, :]
    cur = (pos // SEL_BLOCK)[:, None]
    forced = (blk == 0) | (blk == cur) | (blk == cur - 1)
    imp = jnp.where(forced, BIG, jnp.where(blk <= cur, imp, NEG))
    _, sel_idx = lax.top_k(imp, n_top)
    kb = k_sel.reshape(b, g, n_sel, SEL_BLOCK, d)
    vb = v_sel.reshape(b, g, n_sel, SEL_BLOCK, d)

    def sel_chunk(c):
        q0 = c * SEL_QCHUNK
        qc = lax.dynamic_slice_in_dim(qr, q0, SEL_QCHUNK, axis=3)
        ic = lax.dynamic_slice_in_dim(sel_idx, q0, SEL_QCHUNK, axis=2)
        kg = _gather_blocks(kb, ic).reshape(b, g, SEL_QCHUNK, n_top * SEL_BLOCK, d)
        vg = _gather_blocks(vb, ic).reshape(b, g, SEL_QCHUNK, n_top * SEL_BLOCK, d)
        kpos = (ic[..., None] * SEL_BLOCK + jnp.arange(SEL_BLOCK)).reshape(b, g, SEL_QCHUNK, n_top * SEL_BLOCK)
        qpos = q0 + jnp.arange(SEL_QCHUNK)
        mask = (kpos <= qpos[:, None])[:, :, None]
        p = _masked_softmax(jnp.einsum('bghqd,bgqkd->bghqk', qc, kg) * scale, mask)
        return jnp.einsum('bghqk,bgqkd->bghqd', p.astype(vg.dtype), vg)

    o_sel = _unchunk(lax.map(sel_chunk, jnp.arange(s // SEL_QCHUNK)))

    kw = jnp.pad(k_win, ((0, 0), (0, 0), (WINDOW, 0), (0, 0)))
    vw = jnp.pad(v_win, ((0, 0), (0, 0), (WINDOW, 0), (0, 0)))
    span = WINDOW + WIN_QBLOCK

    def win_block(c):
        q0 = c * WIN_QBLOCK
        qc = lax.dynamic_slice_in_dim(qr, q0, WIN_QBLOCK, axis=3)
        kc_ = lax.dynamic_slice_in_dim(kw, q0, span, axis=2)
        vc_ = lax.dynamic_slice_in_dim(vw, q0, span, axis=2)
        qpos = (q0 + jnp.arange(WIN_QBLOCK))[:, None]
        kpos = (q0 - WINDOW + jnp.arange(span))[None, :]
        mask = (kpos >= 0) & (kpos <= qpos) & (kpos > qpos - WINDOW)
        p = _masked_softmax(jnp.einsum('bghqd,bgkd->bghqk', qc, kc_) * scale, mask)
        return jnp.einsum('bghqk,bgkd->bghqd', p.astype(vc_.dtype), vc_)

    o_win = _unchunk(lax.map(win_block, jnp.arange(s // WIN_QBLOCK)))
    return gates[..., 0:1] * o_cmp + gates[..., 1:2] * o_sel + gates[..., 2:3] * o_win


def moba_attention(q, k, v):
    b, h, s, d = q.shape
    scale = d ** -0.5
    n_blk = -(-s // MOBA_BLOCK)
    s_pad = n_blk * MOBA_BLOCK
    pad = ((0, 0), (0, 0), (0, s_pad - s), (0, 0))
    q, k, v = jnp.pad(q, pad), jnp.pad(k, pad), jnp.pad(v, pad)
    kb = k.reshape(b, h, n_blk, MOBA_BLOCK, d)
    vb = v.reshape(b, h, n_blk, MOBA_BLOCK, d)
    kmean = jnp.mean(kb.astype(jnp.float32), axis=3).astype(k.dtype)
    cur = jnp.arange(s_pad) // MOBA_BLOCK
    past = jnp.arange(n_blk)[None, :] < cur[:, None]
    n_top = min(MOBA_TOPK, n_blk)
    gate = jnp.where(past, jnp.einsum('bhsd,bhnd->bhsn', q, kmean).astype(jnp.float32), NEG)
    _, top_idx = lax.top_k(gate, n_top)
    n_past = n_top * MOBA_BLOCK

    def chunk(c):
        q0 = c * MOBA_QCHUNK
        own0 = (q0 // MOBA_BLOCK) * MOBA_BLOCK
        qc = lax.dynamic_slice_in_dim(q, q0, MOBA_QCHUNK, axis=2)
        ic = lax.dynamic_slice_in_dim(top_idx, q0, MOBA_QCHUNK, axis=2)
        ok = jnp.repeat(ic < q0 // MOBA_BLOCK, MOBA_BLOCK, axis=-1)
        kg = _gather_blocks(kb, ic).reshape(b, h, MOBA_QCHUNK, n_past, d)
        vg = _gather_blocks(vb, ic).reshape(b, h, MOBA_QCHUNK, n_past, d)
        ko = lax.dynamic_slice_in_dim(k, own0, MOBA_BLOCK, axis=2)
        vo = lax.dynamic_slice_in_dim(v, own0, MOBA_BLOCK, axis=2)
        qpos = (q0 + jnp.arange(MOBA_QCHUNK))[:, None]
        kpos = (own0 + jnp.arange(MOBA_BLOCK))[None, :]
        s_past = jnp.where(ok, jnp.einsum('bhqd,bhqkd->bhqk', qc, kg).astype(jnp.float32) * scale, NEG)
        s_own = jnp.where(kpos <= qpos, jnp.einsum('bhqd,bhkd->bhqk', qc, ko).astype(jnp.float32) * scale, NEG)
        p = jax.nn.softmax(jnp.concatenate([s_past, s_own], axis=-1), axis=-1)
        return (jnp.einsum('bhqk,bhqkd->bhqd', p[..., :n_past].astype(vg.dtype), vg)
                + jnp.einsum('bhqk,bhkd->bhqd', p[..., n_past:].astype(vo.dtype), vo))

    o = _unchunk(lax.map(chunk, jnp.arange(s_pad // MOBA_QCHUNK)))
    return o[:, :, :s]


def hybrid_mixer(x, w_in, b_in, cmp_pos, cmp_w1, cmp_b1, cmp_w2, w_proj_a, w_proj_b, w_out, cos, sin):
    b, s, _ = x.shape
    proj = x @ w_in + b_in
    offs = np.cumsum(IN_SPLITS)[:-1].tolist()
    (qa, kca, vca, ksa, vsa, kwa, vwa, ga, qb, kb, vb, merge_a, merge_b) = jnp.split(proj, offs, axis=-1)
    qa = _heads(qa, NSA_HEADS)
    o_a = nsa_attention(
        qa, apply_rope(qa, cos, sin),
        _heads(kca, NSA_KV_HEADS), _heads(vca, NSA_KV_HEADS),
        apply_rope(_heads(ksa, NSA_KV_HEADS), cos, sin), _heads(vsa, NSA_KV_HEADS),
        apply_rope(_heads(kwa, NSA_KV_HEADS), cos, sin), _heads(vwa, NSA_KV_HEADS),
        jax.nn.sigmoid(ga.reshape(b, s, NSA_HEADS, 3).transpose(0, 2, 1, 3)),
        cmp_pos, cmp_w1, cmp_b1, cmp_w2)
    o_b = moba_attention(apply_rope(_heads(qb, MOBA_HEADS), cos, sin),
                         apply_rope(_heads(kb, MOBA_HEADS), cos, sin),
                         _heads(vb, MOBA_HEADS))
    o_a = o_a.transpose(0, 2, 1, 3).reshape(b, s, NSA_Q)
    o_b = o_b.transpose(0, 2, 1, 3).reshape(b, s, MOBA_W)
    merged = jax.nn.sigmoid(merge_a) * (o_a @ w_proj_a) + jax.nn.sigmoid(merge_b) * (o_b @ w_proj_b)
    return merged @ w_out


def moe_ffn(x, router_w, router_bias, w_gate, w_up, w_down, ws_gate, ws_up, ws_down):
    b, s, d = x.shape
    n_tok = b * s
    xf = x.reshape(n_tok, d)
    scores = jax.nn.sigmoid((xf @ router_w).astype(jnp.float32))
    biased = scores + router_bias.astype(jnp.float32)
    grp = biased.reshape(n_tok, N_GROUPS, N_EXPERTS // N_GROUPS)
    grp_score = jnp.sum(lax.top_k(grp, 2)[0], axis=-1)
    _, grp_idx = lax.top_k(grp_score, TOPK_GROUPS)
    grp_mask = jnp.any(grp_idx[..., None] == jnp.arange(N_GROUPS), axis=1)
    expert_mask = jnp.repeat(grp_mask, N_EXPERTS // N_GROUPS, axis=1)
    _, top_idx = lax.top_k(jnp.where(expert_mask, biased, NEG), TOP_K)
    top_s = jnp.take_along_axis(scores, top_idx, axis=-1)
    gate = top_s / jnp.sum(top_s, axis=-1, keepdims=True) * ROUTED_SCALE

    n_assign = n_tok * TOP_K
    e_flat = top_idx.reshape(n_assign)
    tok_flat = jnp.arange(n_assign, dtype=jnp.int32) // TOP_K
    order = jnp.argsort(e_flat)
    e_sorted = e_flat[order]
    counts = jnp.bincount(e_flat, length=N_EXPERTS)
    padded = (counts + DISPATCH_ROWS - 1) // DISPATCH_ROWS * DISPATCH_ROWS
    start = jnp.cumsum(counts) - counts
    pad_end = jnp.cumsum(padded)
    pad_start = pad_end - padded
    dest = pad_start[e_sorted] + jnp.arange(n_assign, dtype=jnp.int32) - start[e_sorted]
    n_blocks = -(-(n_assign + N_EXPERTS * (DISPATCH_ROWS - 1)) // DISPATCH_ROWS)
    n_rows = n_blocks * DISPATCH_ROWS
    row_tok = jnp.full((n_rows,), n_tok, jnp.int32).at[dest].set(tok_flat[order])
    row_gate = jnp.zeros((n_rows,), jnp.float32).at[dest].set(gate.reshape(n_assign)[order])
    blk_expert = jnp.minimum(jnp.searchsorted(pad_end, jnp.arange(n_blocks) * DISPATCH_ROWS, side='right'), N_EXPERTS - 1)
    x_pad = jnp.concatenate([xf, jnp.zeros((1, d), xf.dtype)], axis=0)

    def expert_block(args):
        rows, g, e = args
        xb = x_pad[rows]
        hb = jax.nn.silu(xb @ w_gate[e]) * (xb @ w_up[e])
        return (hb @ w_down[e]) * g[:, None].astype(xb.dtype)

    y_rows = lax.map(expert_block, (row_tok.reshape(n_blocks, DISPATCH_ROWS),
                                    row_gate.reshape(n_blocks, DISPATCH_ROWS), blk_expert))
    routed = jax.ops.segment_sum(y_rows.reshape(n_rows, d), row_tok, num_segments=n_tok + 1)[:n_tok]
    shared = (jax.nn.silu(xf @ ws_gate) * (xf @ ws_up)) @ ws_down
    return (routed + shared).reshape(b, s, d)


def setup_inputs(seed: int = 0) -> dict:
    key = jax.random.key(seed)
    ks = jax.random.split(key, 22)
    f32 = jnp.float32
    D, E, F, L = D_MODEL, N_EXPERTS, D_EXPERT, DEPTH

    def nrm(k, shape, scale):
        return jax.random.normal(k, shape, f32) * scale

    bounds = np.concatenate([[0], np.cumsum(IN_SPLITS)])
    col_scale = np.ones((IN_COLS,), np.float32)
    for i in VALUE_SPLITS:
        col_scale[bounds[i]:bounds[i + 1]] = DEEPNORM_BETA
    return {
        'x': nrm(ks[0], (BATCH, SEQ, D), 1.0),
        'w_in': nrm(ks[1], (L, D, IN_COLS), D ** -0.5) * jnp.asarray(col_scale),
        'b_in': nrm(ks[2], (L, IN_COLS), 0.02),
        'cmp_pos': nrm(ks[3], (L, 2, CMP_BLOCK, HEAD_DIM), 0.02),
        'cmp_w1': nrm(ks[4], (L, 2, CMP_BLOCK * HEAD_DIM, CMP_HIDDEN), (CMP_BLOCK * HEAD_DIM) ** -0.5),
        'cmp_b1': nrm(ks[5], (L, 2, CMP_HIDDEN), 0.02),
        'cmp_w2': nrm(ks[6], (L, 2, CMP_HIDDEN, HEAD_DIM), CMP_HIDDEN ** -0.5),
        'w_proj_a': nrm(ks[7], (L, NSA_Q, D), NSA_Q ** -0.5 * DEEPNORM_BETA),
        'w_proj_b': nrm(ks[8], (L, MOBA_W, D), MOBA_W ** -0.5 * DEEPNORM_BETA),
        'w_out': nrm(ks[9], (L, D, D), D ** -0.5 * DEEPNORM_BETA),
        'ln1_g': 1.0 + nrm(ks[10], (L, D), 0.02),
        'ln1_b': nrm(ks[11], (L, D), 0.02),
        'router_w': nrm(ks[12], (L, D, E), D ** -0.5),
        'router_bias': nrm(ks[13], (L, E), 0.01),
        'w_gate': nrm(ks[14], (L, E, D, F), D ** -0.5),
        'w_up': nrm(ks[15], (L, E, D, F), D ** -0.5),
        'w_down': nrm(ks[16], (L, E, F, D), F ** -0.5 * DEEPNORM_BETA),
        'ws_gate': nrm(ks[17], (L, D, F), D ** -0.5),
        'ws_up': nrm(ks[18], (L, D, F), D ** -0.5),
        'ws_down': nrm(ks[19], (L, F, D), F ** -0.5 * DEEPNORM_BETA),
        'ln2_g': 1.0 + nrm(ks[20], (L, D), 0.02),
        'ln2_b': nrm(ks[21], (L, D), 0.02),
    }


def reference(x, w_in, b_in, cmp_pos, cmp_w1, cmp_b1, cmp_w2, w_proj_a, w_proj_b, w_out, ln1_g, ln1_b,
              router_w, router_bias, w_gate, w_up, w_down, ws_gate, ws_up, ws_down, ln2_g, ln2_b):
    cos, sin = rope_tables(x.shape[1])
    for l in range(DEPTH):
        y = hybrid_mixer(x, w_in[l], b_in[l], cmp_pos[l], cmp_w1[l], cmp_b1[l], cmp_w2[l],
                         w_proj_a[l], w_proj_b[l], w_out[l], cos, sin)
        x = layer_norm(DEEPNORM_ALPHA * x + y, ln1_g[l], ln1_b[l])
        y = moe_ffn(x, router_w[l], router_bias[l], w_gate[l], w_up[l], w_down[l],
                    ws_gate[l], ws_up[l], ws_down[l])
        x = layer_norm(DEEPNORM_ALPHA * x + y, ln2_g[l], ln2_b[l])
    return x
```

```python
import functools

import numpy as np
import jax
import jax.numpy as jnp
from jax import lax
from jax.experimental import pallas as pl
from jax.experimental.pallas import tpu as pltpu

D_MODEL = 1024
HEAD_DIM = 64
NSA_HEADS = 8
NSA_KV_HEADS = 2
MOBA_HEADS = 8
ROPE_THETA = 10000.0
CMP_BLOCK = 32
CMP_STRIDE = 16
CMP_HIDDEN = 128
SEL_BLOCK = 64
SEL_TOPN = 16
WINDOW = 512
MOBA_BLOCK = 256
MOBA_TOPK = 3
N_EXPERTS = 256
TOP_K = 8
N_GROUPS = 8
TOPK_GROUPS = 4
D_EXPERT = 256
ROUTED_SCALE = 2.5
LN_EPS = 1e-5
NEG = -1e30
BIG = 1e30
TAKEN = -3e38

NSA_Q = NSA_HEADS * HEAD_DIM
NSA_KV = NSA_KV_HEADS * HEAD_DIM
MOBA_W = MOBA_HEADS * HEAD_DIM
IN_SPLITS = (NSA_Q, NSA_KV, NSA_KV, NSA_KV, NSA_KV, NSA_KV, NSA_KV, 3 * NSA_HEADS,
             MOBA_W, MOBA_W, MOBA_W, D_MODEL, D_MODEL)

LANES = 128
SUBLANES = 8
SLAB = D_MODEL // LANES
MOBA_SLOTS = LANES // MOBA_HEADS
ROW_BLOCK = 256
VMEM_LIMIT = 52 * 1024 * 1024

F32 = jnp.float32
BF16 = jnp.bfloat16
NT = (((1,), (1,)), ((), ()))


def _cp(sem, vmem=VMEM_LIMIT):
    return pltpu.CompilerParams(dimension_semantics=sem, vmem_limit_bytes=vmem)


def _iota(shape, axis):
    return lax.broadcasted_iota(jnp.int32, shape, axis)


def _inproj_layout():
    offs = np.concatenate([[0], np.cumsum(IN_SPLITS)])
    groups = []
    pieces = []

    def src(i):
        return np.arange(offs[i], offs[i + 1])

    def dup(i):
        c = src(i)
        return np.concatenate([c[:64], c[:64], c[64:], c[64:]])

    col = 0

    def add(cols, outs):
        nonlocal col
        pieces.append(cols)
        groups.append((col, len(cols), outs))
        col += len(cols)

    add(src(0), [(False, BF16), (True, BF16)])
    add(src(1), [(False, F32)])
    add(src(2), [(False, F32)])
    add(dup(3), [(True, BF16)])
    add(dup(4), [(False, BF16)])
    add(dup(5), [(True, BF16)])
    add(dup(6), [(False, BF16)])
    add(np.concatenate([src(7), -np.ones(LANES - 3 * NSA_HEADS, np.int64)]), [(False, F32)])
    add(src(8), [(True, BF16)])
    add(src(9), [(True, BF16)])
    add(src(10), [(False, BF16)])
    add(src(11), [(False, F32)])
    add(src(12), [(False, F32)])
    return groups, np.concatenate(pieces)


_INPROJ_GROUPS, _INPROJ_COLS = _inproj_layout()


def _inproj_kernel(x_ref, w_ref, b_ref, cos_ref, sa_ref, sb_ref, *out_refs):
    xb = x_ref[...].astype(BF16)
    oi = 0
    for c0, width, outs in _INPROJ_GROUPS:
        acc = jnp.dot(xb, w_ref[:, c0:c0 + width], preferred_element_type=F32) + b_ref[:, c0:c0 + width]
        for rope, dtype in outs:
            o_ref = out_refs[oi]
            oi += 1
            if not rope:
                o_ref[...] = acc.astype(dtype)
                continue
            cos = cos_ref[...]
            sa = sa_ref[...]
            sb = sb_ref[...]
            for j in range(0, width, LANES):
                t = acc[:, j:j + LANES]
                r = t * cos + pltpu.roll(t, LANES - HEAD_DIM // 2, 1) * sa + pltpu.roll(t, HEAD_DIM // 2, 1) * sb
                o_ref[:, j:j + LANES] = r.astype(dtype)


def _inproj(x2, w, b, cos, sa, sb, seq, tm=256):
    n = x2.shape[0]
    ctot = w.shape[1]
    nt_seq = seq // tm
    out_shape, out_specs = [], []
    for _, width, outs in _INPROJ_GROUPS:
        for _, dtype in outs:
            out_shape.append(jax.ShapeDtypeStruct((n, width), dtype))
            out_specs.append(pl.BlockSpec((tm, width), lambda i: (i, 0)))
    tab = pl.BlockSpec((tm, LANES), lambda i: (i % nt_seq, 0))
    return pl.pallas_call(
        _inproj_kernel,
        grid=(n // tm,),
        in_specs=[pl.BlockSpec((tm, D_MODEL), lambda i: (i, 0)),
                  pl.BlockSpec((D_MODEL, ctot), lambda i: (0, 0)),
                  pl.BlockSpec((1, ctot), lambda i: (0, 0)),
                  tab, tab, tab],
        out_specs=out_specs,
        out_shape=out_shape,
        compiler_params=_cp(("parallel",)),
        name="inproj",
    )(x2, w, b, cos, sa, sb)


def _compress_kernel(kc_ref, vc_ref, ptop_ref, pbot_ref, wtop_ref, wbot_ref, b1_ref, w2_ref, ko_ref, vo_ref):
    ncb = kc_ref.shape[1]
    for kv, (c_ref, o_ref) in enumerate(((kc_ref, ko_ref), (vc_ref, vo_ref))):
        c = c_ref[0]
        a = jnp.dot((c + ptop_ref[kv]).astype(BF16), wtop_ref[kv], preferred_element_type=F32)
        bm = jnp.dot((c + pbot_ref[kv]).astype(BF16), wbot_ref[kv], preferred_element_type=F32)
        hid = jax.nn.gelu(a + pltpu.roll(bm, ncb - 1, 0) + b1_ref[kv])
        for g in range(NSA_KV_HEADS):
            hg = hid[:, g * CMP_HIDDEN:(g + 1) * CMP_HIDDEN].astype(BF16)
            o_ref[0, g] = jnp.dot(hg, w2_ref[kv], preferred_element_type=F32).astype(BF16)


def _compress(kc3, vc3, ptop, pbot, wtop, wbot, b1, w2):
    bsz, ncb, cw = kc3.shape
    hw = NSA_KV_HEADS * CMP_HIDDEN
    full = lambda shp: pl.BlockSpec(shp, lambda b: (0,) * len(shp))
    out = jax.ShapeDtypeStruct((bsz, NSA_KV_HEADS, ncb, LANES), BF16)
    ospec = pl.BlockSpec((1, NSA_KV_HEADS, ncb, LANES), lambda b: (b, 0, 0, 0))
    return pl.pallas_call(
        _compress_kernel,
        grid=(bsz,),
        in_specs=[pl.BlockSpec((1, ncb, cw), lambda b: (b, 0, 0)),
                  pl.BlockSpec((1, ncb, cw), lambda b: (b, 0, 0)),
                  full((2, 1, cw)), full((2, 1, cw)),
                  full((2, cw, hw)), full((2, cw, hw)),
                  full((2, 1, hw)), full((2, CMP_HIDDEN, LANES))],
        out_specs=[ospec, ospec],
        out_shape=[out, out],
        compiler_params=_cp(("parallel",)),
        name="nsa_compress",
    )(kc3, vc3, ptop, pbot, wtop, wbot, b1, w2)


def _rank_rows(val, n_rows):
    rows = _iota(val.shape, 0)
    rank = jnp.zeros(val.shape, F32)
    for i in range(n_rows):
        vi = val[i:i + 1, :]
        ahead = (vi > val) | ((vi == val) & (rows > i))
        rank = rank + jnp.where(ahead, 1.0, 0.0)
    return rank


def _transpose_to_rows(mt):
    tq = mt.shape[1]
    return jnp.concatenate([mt[:, c:c + LANES].T for c in range(0, tq, LANES)], axis=0)


def _nsa_cmp_kernel(q_ref, kc_ref, vc_ref, ov_ref, o_ref, mem_ref, *, tq, n_cmp, n_sel):
    i = pl.program_id(2)
    ncb = kc_ref.shape[2]
    kc = kc_ref[0, 0]
    vc = vc_ref[0, 0]
    pos = i * tq + _iota((tq, ncb), 0)
    blk = _iota((tq, ncb), 1)
    valid = (blk * CMP_STRIDE + CMP_BLOCK - 1 <= pos) & (blk < n_cmp)
    anyv = (i * tq + _iota((tq, 1), 0) >= CMP_BLOCK - 1).astype(F32)
    lane = _iota((1, LANES), 1)
    scale = HEAD_DIM ** -0.5
    psum = jnp.zeros((tq, ncb), F32)
    outs = []
    hg = NSA_HEADS // NSA_KV_HEADS
    for t in range(hg):
        qb = q_ref[:, (t // 2) * LANES:(t // 2 + 1) * LANES]
        half = (lane < HEAD_DIM) if t % 2 == 0 else (lane >= HEAD_DIM)
        qm = jnp.where(half, qb, jnp.zeros_like(qb))
        s = lax.dot_general(qm, kc, NT, preferred_element_type=F32) * scale
        s = jnp.where(valid, s, NEG)
        e = jnp.exp(s - jnp.max(s, axis=1, keepdims=True))
        p = e / jnp.sum(e, axis=1, keepdims=True) * anyv
        psum = psum + p
        outs.append(jnp.dot(p.astype(BF16), vc, preferred_element_type=F32))
    for r in range(hg // 2):
        o_ref[:, r * LANES:(r + 1) * LANES] = jnp.where(lane < HEAD_DIM, outs[2 * r], outs[2 * r + 1]).astype(BF16)

    imp = lax.dot_general(ov_ref[...], psum, NT, preferred_element_type=F32, precision=lax.Precision.HIGHEST)
    imp = imp[:n_sel]
    j = _iota((n_sel, tq), 0)
    cur = (i * tq + _iota((n_sel, tq), 1)) // SEL_BLOCK
    forced = (j == 0) | (j == cur) | (j == cur - 1)
    val = jnp.where(forced, BIG, jnp.where(j <= cur, imp, NEG))
    rank = _rank_rows(val, n_sel)
    member = jnp.where((rank < float(min(SEL_TOPN, n_sel))) & (j <= cur), 1.0, 0.0)
    if n_sel < LANES:
        member = jnp.concatenate([member, jnp.zeros((LANES - n_sel, tq), F32)], axis=0)
    mem_ref[0, 0] = _transpose_to_rows(member).astype(BF16)


def _nsa_cmp(qn, kcmp, vcmp, ovt, bsz, seq, tq=256):
    n = qn.shape[0]
    nq = seq // tq
    ncb = kcmp.shape[2]
    n_sel = seq // SEL_BLOCK
    gw = NSA_Q // NSA_KV_HEADS
    kern = functools.partial(_nsa_cmp_kernel, tq=tq, n_cmp=ncb - 1, n_sel=n_sel)
    kvspec = pl.BlockSpec((1, 1, ncb, LANES), lambda b, g, i: (b, g, 0, 0))
    return pl.pallas_call(
        kern,
        grid=(bsz, NSA_KV_HEADS, nq),
        in_specs=[pl.BlockSpec((tq, gw), lambda b, g, i: (b * nq + i, g)),
                  kvspec, kvspec,
                  pl.BlockSpec((LANES, ncb), lambda b, g, i: (0, 0))],
        out_specs=[pl.BlockSpec((tq, gw), lambda b, g, i: (b * nq + i, g)),
                   pl.BlockSpec((1, 1, tq, LANES), lambda b, g, i: (b, g, i, 0))],
        out_shape=[jax.ShapeDtypeStruct((n, NSA_Q), BF16),
                   jax.ShapeDtypeStruct((bsz, NSA_KV_HEADS, seq, LANES), BF16)],
        compiler_params=_cp(("parallel", "parallel", "parallel")),
        name="nsa_cmp_select",
    )(qn, kcmp, vcmp, ovt)


def _moba_kmean_kernel(k_ref, o_ref):
    nblk = o_ref.shape[1]
    k = k_ref[...].astype(F32).reshape(nblk, MOBA_BLOCK, MOBA_W)
    o_ref[0] = jnp.sum(k, axis=1) * (1.0 / MOBA_BLOCK)


def _moba_kmean(kb, bsz, seq):
    nblk = seq // MOBA_BLOCK
    return pl.pallas_call(
        _moba_kmean_kernel,
        grid=(bsz,),
        in_specs=[pl.BlockSpec((seq, MOBA_W), lambda b: (b, 0))],
        out_specs=pl.BlockSpec((1, nblk, MOBA_W), lambda b: (b, 0, 0)),
        out_shape=jax.ShapeDtypeStruct((bsz, nblk, MOBA_W), F32),
        compiler_params=_cp(("parallel",)),
        name="moba_kmean",
    )(kb)


def _moba_select_kernel(q_ref, km_ref, mem_ref, *, tq, nblk):
    i = pl.program_id(1)
    km = km_ref[0]
    if nblk < MOBA_SLOTS:
        km = jnp.concatenate([km, jnp.zeros((MOBA_SLOTS - nblk, MOBA_W), F32)], axis=0)
    kmt = jnp.concatenate([km] * MOBA_HEADS, axis=0)
    r = _iota((LANES, MOBA_W), 0)
    c = _iota((LANES, MOBA_W), 1)
    kmt = jnp.where(r // MOBA_SLOTS == c // HEAD_DIM, kmt, 0.0).astype(BF16)
    gate = lax.dot_general(kmt, q_ref[...], NT, preferred_element_type=F32)
    members = []
    for h in range(MOBA_HEADS):
        sub = gate[h * MOBA_SLOTS:(h + 1) * MOBA_SLOTS]
        j = _iota((MOBA_SLOTS, tq), 0)
        cur = (i * tq + _iota((MOBA_SLOTS, tq), 1)) // MOBA_BLOCK
        past = j < cur
        val = jnp.where(past, sub, NEG)
        rank = _rank_rows(val, min(nblk, MOBA_SLOTS))
        members.append(jnp.where(((rank < float(MOBA_TOPK)) & past) | (j == cur), 1.0, 0.0))
    mem_ref[...] = _transpose_to_rows(jnp.concatenate(members, axis=0)).astype(BF16)


def _moba_select(qb, kmean, bsz, seq, tq=256):
    n = qb.shape[0]
    nq = seq // tq
    nblk = seq // MOBA_BLOCK
    kern = functools.partial(_moba_select_kernel, tq=tq, nblk=nblk)
    return pl.pallas_call(
        kern,
        grid=(bsz, nq),
        in_specs=[pl.BlockSpec((tq, MOBA_W), lambda b, i: (b * nq + i, 0)),
                  pl.BlockSpec((1, nblk, MOBA_W), lambda b, i: (b, 0, 0))],
        out_specs=pl.BlockSpec((tq, LANES), lambda b, i: (b * nq + i, 0)),
        out_shape=jax.ShapeDtypeStruct((n, LANES), BF16),
        compiler_params=_cp(("parallel", "parallel")),
        name="moba_select",
    )(qb, kmean)


def _flash_kernel(qi_ref, kj_ref, first_ref, last_ref, q_ref, k_ref, v_ref, mem_ref, o_ref,
                  m_sc, l_sc, acc_sc, *, mode, nhq, tile):
    hg = pl.program_id(1)
    p = pl.program_id(2)
    i = qi_ref[p]
    jj = kj_ref[p]

    @pl.when(first_ref[p] == 1)
    def _():
        m_sc[...] = jnp.full(m_sc.shape, NEG, F32)
        l_sc[...] = jnp.zeros(l_sc.shape, F32)
        acc_sc[...] = jnp.zeros(acc_sc.shape, F32)

    qpos = i * tile + _iota((tile, tile), 0)
    kpos = jj * tile + _iota((tile, tile), 1)
    mask = kpos <= qpos
    if mode == "win":
        mask = mask & (kpos > qpos - WINDOW)
    er = _iota((LANES, tile), 0)
    ekb = (jj * tile + _iota((LANES, tile), 1))
    if mode == "sel":
        expand = jnp.where(er == ekb // SEL_BLOCK, 1.0, 0.0).astype(BF16)
        mask = mask & (jnp.dot(mem_ref[0, 0], expand, preferred_element_type=F32) > 0.5)
    k = k_ref[...]
    v = v_ref[...]
    lane = _iota((1, LANES), 1)
    for t in range(nhq):
        qb = q_ref[:, (t // 2) * LANES:(t // 2 + 1) * LANES]
        half = (lane < HEAD_DIM) if t % 2 == 0 else (lane >= HEAD_DIM)
        qm = jnp.where(half, qb, jnp.zeros_like(qb)) * jnp.asarray(HEAD_DIM ** -0.5, BF16)
        s = lax.dot_general(qm, k, NT, preferred_element_type=F32)
        if mode == "moba":
            head = hg * 2 + t
            expand = jnp.where(er == head * MOBA_SLOTS + ekb // MOBA_BLOCK, 1.0, 0.0).astype(BF16)
            mt = mask & (jnp.dot(mem_ref[...], expand, preferred_element_type=F32) > 0.5)
        else:
            mt = mask
        s = jnp.where(mt, s, NEG)
        m_old = m_sc[t]
        m_new = jnp.maximum(m_old, jnp.max(s, axis=1, keepdims=True))
        alpha = jnp.exp(m_old - m_new)
        pr = jnp.exp(s - m_new)
        l_sc[t] = alpha * l_sc[t] + jnp.sum(pr, axis=1, keepdims=True)
        acc_sc[t] = alpha * acc_sc[t] + jnp.dot(pr.astype(BF16), v, preferred_element_type=F32)
        m_sc[t] = m_new

    @pl.when(last_ref[p] == 1)
    def _():
        for r in range(nhq // 2):
            o0 = acc_sc[2 * r] / l_sc[2 * r]
            o1 = acc_sc[2 * r + 1] / l_sc[2 * r + 1]
            o_ref[:, r * LANES:(r + 1) * LANES] = jnp.where(lane < HEAD_DIM, o0, o1).astype(BF16)


def _pair_tables(nq, lo_fn):
    qi, kj, first, last = [], [], [], []
    for i in range(nq):
        lo = lo_fn(i)
        for j in range(lo, i + 1):
            qi.append(i)
            kj.append(j)
            first.append(int(j == lo))
            last.append(int(j == i))
    mk = lambda a: jnp.asarray(np.asarray(a, np.int32))
    return mk(qi), mk(kj), mk(first), mk(last)


def _flash(mode, q, k, v, mem, bsz, seq, tile=256):
    n = q.shape[0]
    nq = seq // tile
    if mode == "moba":
        ngrp, nhq = MOBA_HEADS // 2, 2
        mem_spec = pl.BlockSpec((tile, LANES), lambda b, g, p, qi, kj, fi, la: (b * nq + qi[p], 0))
    else:
        ngrp, nhq = NSA_KV_HEADS, NSA_HEADS // NSA_KV_HEADS
        mem_spec = pl.BlockSpec((1, 1, tile, LANES), lambda b, g, p, qi, kj, fi, la: (b, g, qi[p], 0))
    lo_fn = (lambda i: max(0, i - WINDOW // tile)) if mode == "win" else (lambda i: 0)
    tables = _pair_tables(nq, lo_fn)
    npairs = int(tables[0].shape[0])
    qw = nhq // 2 * LANES
    kern = functools.partial(_flash_kernel, mode=mode, nhq=nhq, tile=tile)
    kvspec = pl.BlockSpec((tile, LANES), lambda b, g, p, qi, kj, fi, la: (b * nq + kj[p], g))
    qspec = pl.BlockSpec((tile, qw), lambda b, g, p, qi, kj, fi, la: (b * nq + qi[p], g))
    grid_spec = pltpu.PrefetchScalarGridSpec(
        num_scalar_prefetch=4,
        grid=(bsz, ngrp, npairs),
        in_specs=[qspec, kvspec, kvspec, mem_spec],
        out_specs=qspec,
        scratch_shapes=[pltpu.VMEM((nhq, tile, 1), F32), pltpu.VMEM((nhq, tile, 1), F32),
                        pltpu.VMEM((nhq, tile, LANES), F32)],
    )
    return pl.pallas_call(
        kern,
        grid_spec=grid_spec,
        out_shape=jax.ShapeDtypeStruct((n, ngrp * qw), BF16),
        compiler_params=_cp(("parallel", "parallel", "arbitrary")),
        name="flash_" + mode,
    )(*tables, q, k, v, mem)


def _layer_norm(z, g, b):
    mu = jnp.mean(z, axis=-1, keepdims=True)
    zc = z - mu
    var = jnp.mean(zc * zc, axis=-1, keepdims=True)
    return zc * lax.rsqrt(var + LN_EPS) * g + b


def _store_slabs(slab_ref, val, rows):
    for c in range(SLAB):
        slab_ref[pl.ds(c, rows, stride=SLAB), :] = val[:, c * LANES:(c + 1) * LANES]


def _post_kernel(ocmp_ref, osel_ref, owin_ref, ob_ref, g_ref, ma_ref, mb_ref, x_ref,
                 eg_ref, wpa_ref, wpb_ref, wout_ref, lng_ref, lnb_ref, x1_ref, x1s_ref, *, alpha, tm):
    sig = jax.nn.sigmoid(g_ref[...])
    oa = jnp.zeros((tm, NSA_Q), F32)
    for c, o_ref in enumerate((ocmp_ref, osel_ref, owin_ref)):
        gexp = jnp.dot(sig, eg_ref[c], preferred_element_type=F32, precision=lax.Precision.HIGHEST)
        oa = oa + gexp * o_ref[...].astype(F32)
    pa = jnp.dot(oa.astype(BF16), wpa_ref[...], preferred_element_type=F32)
    pb = jnp.dot(ob_ref[...], wpb_ref[...], preferred_element_type=F32)
    merged = jax.nn.sigmoid(ma_ref[...]) * pa + jax.nn.sigmoid(mb_ref[...]) * pb
    y = jnp.dot(merged.astype(BF16), wout_ref[...], preferred_element_type=F32)
    x1 = _layer_norm(alpha * x_ref[...] + y, lng_ref[...], lnb_ref[...])
    x1_ref[...] = x1
    _store_slabs(x1s_ref, x1, tm)


def _post(ocmp, osel, owin, ob, gates, ma, mb, x2, eg, wpa, wpb, wout, lng, lnb, alpha, tm=256):
    n = x2.shape[0]
    row = lambda w: pl.BlockSpec((tm, w), lambda i: (i, 0))
    full = lambda shp: pl.BlockSpec(shp, lambda i: (0,) * len(shp))
    kern = functools.partial(_post_kernel, alpha=alpha, tm=tm)
    return pl.pallas_call(
        kern,
        grid=(n // tm,),
        in_specs=[row(NSA_Q), row(NSA_Q), row(NSA_Q), row(MOBA_W), row(LANES), row(D_MODEL), row(D_MODEL),
                  row(D_MODEL), full((3, LANES, NSA_Q)), full((NSA_Q, D_MODEL)), full((MOBA_W, D_MODEL)),
                  full((D_MODEL, D_MODEL)), full((1, D_MODEL)), full((1, D_MODEL))],
        out_specs=[row(D_MODEL), pl.BlockSpec((tm * SLAB, LANES), lambda i: (i, 0))],
        out_shape=[jax.ShapeDtypeStruct((n, D_MODEL), F32), jax.ShapeDtypeStruct((n * SLAB, LANES), F32)],
        compiler_params=_cp(("parallel",)),
        name="post_attention",
    )(ocmp, osel, owin, ob, gates, ma, mb, x2, eg, wpa, wpb, wout, lng, lnb)


def _first_argmax_rows(cur, rows, n_rows):
    mx = jnp.max(cur, axis=0, keepdims=True)
    idx = jnp.min(jnp.where(cur == mx, rows, n_rows), axis=0, keepdims=True)
    return mx, idx


def _router_kernel(x_ref, rwt_ref, bias_ref, idx_ref, gate_ref, rank_ref, cnt_ref, carry_sc, *, tm):
    i = pl.program_id(0)

    @pl.when(i == 0)
    def _():
        carry_sc[...] = jnp.zeros(carry_sc.shape, F32)

    logits = lax.dot_general(rwt_ref[...], x_ref[...].astype(BF16), NT, preferred_element_type=F32)
    sc = jax.nn.sigmoid(logits)
    biased = sc + bias_ref[...]
    gsz = N_EXPERTS // N_GROUPS
    grow = _iota((gsz, tm), 0)
    gscore = []
    for g in range(N_GROUPS):
        grp = biased[g * gsz:(g + 1) * gsz]
        m1, i1 = _first_argmax_rows(grp, grow, gsz)
        m2 = jnp.max(jnp.where(grow == i1, TAKEN, grp), axis=0, keepdims=True)
        gscore.append(m1 + m2)
    gs = jnp.concatenate(gscore, axis=0)
    keep = _rank_rows(gs, N_GROUPS) < float(TOPK_GROUPS)
    cur = jnp.concatenate(
        [jnp.where(keep[g:g + 1], biased[g * gsz:(g + 1) * gsz], NEG) for g in range(N_GROUPS)], axis=0)
    rows = _iota((N_EXPERTS, tm), 0)
    sels, tops, idxs = [], [], []
    for _ in range(TOP_K):
        _, ik = _first_argmax_rows(cur, rows, N_EXPERTS)
        selk = rows == ik
        tops.append(jnp.sum(jnp.where(selk, sc, 0.0), axis=0, keepdims=True))
        cur = jnp.where(selk, TAKEN, cur)
        sels.append(selk)
        idxs.append(ik)
    denom = tops[0]
    for tk_ in tops[1:]:
        denom = denom + tk_
    selall = jnp.where(cur == TAKEN, 1.0, 0.0)
    upper = jnp.where(_iota((tm, tm), 0) < _iota((tm, tm), 1), 1.0, 0.0).astype(BF16)
    before = jnp.dot(selall.astype(BF16), upper, preferred_element_type=F32) + carry_sc[...]
    idx_ref[...] = jnp.concatenate(idxs, axis=0)
    gate_ref[...] = jnp.concatenate([tk_ / denom * ROUTED_SCALE for tk_ in tops], axis=0)
    rank_ref[...] = jnp.concatenate(
        [jnp.sum(jnp.where(s, before, 0.0), axis=0, keepdims=True) for s in sels], axis=0).astype(jnp.int32)
    carry_sc[...] = carry_sc[...] + jnp.sum(selall, axis=1, keepdims=True)
    cnt_ref[...] = jnp.broadcast_to(carry_sc[...], cnt_ref.shape)


def _router(x1, rwt, bias, tm=256):
    n = x1.shape[0]
    kspec = pl.BlockSpec((TOP_K, tm), lambda i: (0, i))
    return pl.pallas_call(
        functools.partial(_router_kernel, tm=tm),
        grid=(n // tm,),
        in_specs=[pl.BlockSpec((tm, D_MODEL), lambda i: (i, 0)),
                  pl.BlockSpec((N_EXPERTS, D_MODEL), lambda i: (0, 0)),
                  pl.BlockSpec((N_EXPERTS, 1), lambda i: (0, 0))],
        out_specs=[kspec, kspec, kspec, pl.BlockSpec((N_EXPERTS, LANES), lambda i: (0, 0))],
        out_shape=[jax.ShapeDtypeStruct((TOP_K, n), jnp.int32), jax.ShapeDtypeStruct((TOP_K, n), F32),
                   jax.ShapeDtypeStruct((TOP_K, n), jnp.int32), jax.ShapeDtypeStruct((N_EXPERTS, LANES), F32)],
        scratch_shapes=[pltpu.VMEM((N_EXPERTS, 1), F32)],
        compiler_params=_cp(("arbitrary",)),
        name="router",
    )(x1, rwt, bias)


def _expert_kernel(be_ref, tok_hbm, x_hbm, gate_ref, wg_ref, wu_ref, wd_ref, y_ref,
                   idx_sm, xbuf, xmat, wgu_sc, wd_sc, idx_sem, row_sem, *, nblk):
    i = pl.program_id(0)
    slot = i % 2
    rb = ROW_BLOCK

    def idx_copy(blk, s):
        return pltpu.make_async_copy(tok_hbm.at[blk], idx_sm.at[s], idx_sem.at[s])

    def row_copy(tok, r, s):
        return pltpu.make_async_copy(x_hbm.at[pl.ds(pl.multiple_of(tok * SLAB, SLAB), SLAB)],
                                     xbuf.at[s, pl.ds(r * SLAB, SLAB)], row_sem.at[s])

    def gather(s):
        def body(r, carry):
            row_copy(idx_sm[s, r], r, s).start()
            return carry
        lax.fori_loop(0, rb, body, 0)

    @pl.when(i == 0)
    def _():
        idx_copy(0, 0).start()
        idx_copy(0, 0).wait()
        gather(0)
        if nblk > 1:
            idx_copy(1, 1).start()

    @pl.when(i + 1 < nblk)
    def _():
        idx_copy(i + 1, 1 - slot).wait()
        gather(1 - slot)

    @pl.when(i + 2 < nblk)
    def _():
        idx_copy(i + 2, slot).start()

    e = be_ref[i]
    prev = be_ref[jnp.maximum(i - 1, 0)]

    @pl.when((i == 0) | (e != prev))
    def _():
        wgu_sc[:, :D_EXPERT] = wg_ref[0, 0].astype(BF16)
        wgu_sc[:, D_EXPERT:] = wu_ref[0, 0].astype(BF16)
        wd_sc[...] = wd_ref[0, 0].astype(BF16)

    pltpu.make_async_copy(x_hbm.at[pl.ds(0, rb * SLAB)], xbuf.at[slot], row_sem.at[slot]).wait()
    for c in range(SLAB):
        xmat[:, c * LANES:(c + 1) * LANES] = xbuf[slot, pl.ds(c, rb, stride=SLAB), :].astype(BF16)
    hgu = jnp.dot(xmat[...], wgu_sc[...], preferred_element_type=F32)
    h = jax.nn.silu(hgu[:, :D_EXPERT]) * hgu[:, D_EXPERT:]
    y = jnp.dot(h.astype(BF16), wd_sc[...], preferred_element_type=F32)
    gcol = _transpose_to_rows(jnp.broadcast_to(gate_ref[0], (LANES, rb)))
    for c in range(SLAB):
        y_ref[pl.ds(c, rb, stride=SLAB), :] = y[:, c * LANES:(c + 1) * LANES] * gcol


def _experts(blk_expert, row_tok, x1s, row_gate, w_gate, w_up, w_down, layer):
    nblk = row_tok.shape[0]
    rb = ROW_BLOCK
    kern = functools.partial(_expert_kernel, nblk=nblk)
    grid_spec = pltpu.PrefetchScalarGridSpec(
        num_scalar_prefetch=1,
        grid=(nblk,),
        in_specs=[pl.BlockSpec(memory_space=pl.ANY),
                  pl.BlockSpec(memory_space=pl.ANY),
                  pl.BlockSpec((1, 1, rb), lambda i, be: (i, 0, 0)),
                  pl.BlockSpec((1, 1, D_MODEL, D_EXPERT), lambda i, be: (layer, be[i], 0, 0)),
                  pl.BlockSpec((1, 1, D_MODEL, D_EXPERT), lambda i, be: (layer, be[i], 0, 0)),
                  pl.BlockSpec((1, 1, D_EXPERT, D_MODEL), lambda i, be: (layer, be[i], 0, 0))],
        out_specs=pl.BlockSpec((rb * SLAB, LANES), lambda i, be: (i, 0)),
        scratch_shapes=[pltpu.SMEM((2, rb), jnp.int32),
                        pltpu.VMEM((2, rb * SLAB, LANES), F32),
                        pltpu.VMEM((rb, D_MODEL), BF16),
                        pltpu.VMEM((D_MODEL, 2 * D_EXPERT), BF16),
                        pltpu.VMEM((D_EXPERT, D_MODEL), BF16),
                        pltpu.SemaphoreType.DMA((2,)),
                        pltpu.SemaphoreType.DMA((2,))],
    )
    return pl.pallas_call(
        kern,
        grid_spec=grid_spec,
        out_shape=jax.ShapeDtypeStruct((nblk * rb * SLAB, LANES), F32),
        compiler_params=_cp(("arbitrary",)),
        name="moe_experts",
    )(blk_expert, row_tok, x1s, row_gate, w_gate, w_up, w_down)


def _combine_kernel(dest_hbm, y_hbm, x1_ref, wsg_ref, wsu_ref, wsd_ref, lng_ref, lnb_ref, o_ref,
                    idx_sm, ybuf, acc_sc, idx_sem, row_sem, *, ntile, tm, alpha):
    i = pl.program_id(0)
    slot = i % 2
    nrow = TOP_K * tm

    def idx_copy(t, s):
        return pltpu.make_async_copy(dest_hbm.at[t], idx_sm.at[s], idx_sem.at[s])

    def gather(s):
        def body(r, carry):
            src = y_hbm.at[pl.ds(pl.multiple_of(idx_sm[s, r] * SLAB, SLAB), SLAB)]
            pltpu.make_async_copy(src, ybuf.at[s, pl.ds(r * SLAB, SLAB)], row_sem.at[s]).start()
            return carry
        lax.fori_loop(0, nrow, body, 0)

    @pl.when(i == 0)
    def _():
        idx_copy(0, 0).start()
        idx_copy(0, 0).wait()
        gather(0)
        if ntile > 1:
            idx_copy(1, 1).start()

    @pl.when(i + 1 < ntile)
    def _():
        idx_copy(i + 1, 1 - slot).wait()
        gather(1 - slot)

    @pl.when(i + 2 < ntile)
    def _():
        idx_copy(i + 2, slot).start()

    x1 = x1_ref[...]
    xb = x1.astype(BF16)
    hs = jax.nn.silu(jnp.dot(xb, wsg_ref[...], preferred_element_type=F32)) * jnp.dot(
        xb, wsu_ref[...], preferred_element_type=F32)
    shared = jnp.dot(hs.astype(BF16), wsd_ref[...], preferred_element_type=F32)

    pltpu.make_async_copy(y_hbm.at[pl.ds(0, nrow * SLAB)], ybuf.at[slot], row_sem.at[slot]).wait()
    acc = ybuf[slot, pl.ds(0, tm * SLAB), :]
    for k in range(1, TOP_K):
        acc = acc + ybuf[slot, pl.ds(k * tm * SLAB, tm * SLAB), :]
    acc_sc[...] = acc
    routed = jnp.concatenate([acc_sc[pl.ds(c, tm, stride=SLAB), :] for c in range(SLAB)], axis=1)
    o_ref[...] = _layer_norm(alpha * x1 + routed + shared, lng_ref[...], lnb_ref[...])


def _combine(dest_tiles, y_slab, x1, wsg, wsu, wsd, lng, lnb, alpha, tm=128):
    n = x1.shape[0]
    ntile = n // tm
    nrow = TOP_K * tm
    full = lambda shp: pl.BlockSpec(shp, lambda i: (0,) * len(shp))
    kern = functools.partial(_combine_kernel, ntile=ntile, tm=tm, alpha=alpha)
    return pl.pallas_call(
        kern,
        grid=(ntile,),
        in_specs=[pl.BlockSpec(memory_space=pl.ANY),
                  pl.BlockSpec(memory_space=pl.ANY),
                  pl.BlockSpec((tm, D_MODEL), lambda i: (i, 0)),
                  full((D_MODEL, D_EXPERT)), full((D_MODEL, D_EXPERT)), full((D_EXPERT, D_MODEL)),
                  full((1, D_MODEL)), full((1, D_MODEL))],
        out_specs=pl.BlockSpec((tm, D_MODEL), lambda i: (i, 0)),
        out_shape=jax.ShapeDtypeStruct((n, D_MODEL), F32),
        scratch_shapes=[pltpu.SMEM((2, nrow), jnp.int32),
                        pltpu.VMEM((2, nrow * SLAB, LANES), F32),
                        pltpu.VMEM((tm * SLAB, LANES), F32),
                        pltpu.SemaphoreType.DMA((2,)),
                        pltpu.SemaphoreType.DMA((2,))],
        compiler_params=_cp(("arbitrary",)),
        name="moe_combine",
    )(dest_tiles, y_slab, x1, wsg, wsu, wsd, lng, lnb)


def _rope_tables(seq):
    pos = jnp.arange(seq, dtype=F32)
    inv = ROPE_THETA ** (-jnp.arange(0, HEAD_DIM, 2, dtype=F32) / HEAD_DIM)
    ang = pos[:, None] * inv[None, :]
    ang = jnp.concatenate([ang] * (LANES // (HEAD_DIM // 2)), axis=-1)
    first = (np.arange(LANES) % HEAD_DIM) < HEAD_DIM // 2
    sin = jnp.sin(ang)
    return jnp.cos(ang), jnp.where(first, -sin, 0.0), jnp.where(first, 0.0, sin)


def _pack_inproj(w_in, b_in):
    cols = _INPROJ_COLS
    safe = np.where(cols < 0, 0, cols)
    live = jnp.asarray(cols >= 0)
    w = jnp.where(live[None, :], w_in[:, safe], 0.0).astype(BF16)
    b = jnp.where(live, b_in[safe], 0.0)[None, :]
    return w, b


def _pack_compress(cmp_pos, cmp_w1, cmp_b1, cmp_w2):
    half = CMP_BLOCK // 2
    g = NSA_KV_HEADS

    def w_half(w1, lo):
        w = w1.reshape(2, CMP_BLOCK, HEAD_DIM, CMP_HIDDEN)[:, lo:lo + half]
        eye = jnp.eye(g, dtype=F32)
        w = jnp.einsum("klde,gh->klgdhe", w, eye)
        return w.reshape(2, half * g * HEAD_DIM, g * CMP_HIDDEN).astype(BF16)

    def p_half(lo):
        p = cmp_pos[:, lo:lo + half]
        p = jnp.broadcast_to(p[:, :, None, :], (2, half, g, HEAD_DIM))
        return p.reshape(2, 1, half * g * HEAD_DIM)

    b1 = jnp.concatenate([cmp_b1] * g, axis=-1)[:, None, :]
    w2 = jnp.concatenate([cmp_w2, cmp_w2], axis=-1).astype(BF16)
    return p_half(0), p_half(half), w_half(cmp_w1, 0), w_half(cmp_w1, half), b1, w2


def _overlap_t(seq):
    ncb = seq // CMP_STRIDE
    n_cmp = ncb - 1
    n_sel = seq // SEL_BLOCK
    c0 = np.arange(ncb)[None, :] * CMP_STRIDE
    s0 = np.arange(LANES)[:, None] * SEL_BLOCK
    ov = np.clip(np.minimum(c0 + CMP_BLOCK, s0 + SEL_BLOCK) - np.maximum(c0, s0), 0, None) / CMP_STRIDE
    ov = ov * (np.arange(ncb)[None, :] < n_cmp) * (np.arange(LANES)[:, None] < n_sel)
    return jnp.asarray(ov, F32)


def _gate_expand():
    r = np.arange(LANES)[None, :, None]
    col = np.arange(NSA_Q)[None, None, :]
    c = np.arange(3)[:, None, None]
    return jnp.asarray(r == (col // HEAD_DIM) * 3 + c, F32)


def _dispatch_plan(top_idx, gate, rank, counts, n_tok):
    rb = ROW_BLOCK
    n_assign = n_tok * TOP_K
    nblk = -(-(n_assign + N_EXPERTS * (rb - 1)) // rb)
    padded = (counts + rb - 1) // rb * rb
    pad_end = jnp.cumsum(padded)
    pad_start = pad_end - padded
    dest = pad_start[top_idx] + rank
    tok = jnp.broadcast_to(jnp.arange(n_tok, dtype=jnp.int32)[None, :], dest.shape)
    row_tok = jnp.zeros((nblk * rb,), jnp.int32).at[dest.reshape(-1)].set(tok.reshape(-1))
    row_gate = jnp.zeros((nblk * rb,), F32).at[dest.reshape(-1)].set(gate.reshape(-1))
    blk_expert = jnp.minimum(jnp.searchsorted(pad_end, jnp.arange(nblk) * rb, side="right"), N_EXPERTS - 1)
    return dest, row_tok.reshape(nblk, rb), row_gate.reshape(nblk, 1, rb), blk_expert.astype(jnp.int32)


def kernel(x, w_in, b_in, cmp_pos, cmp_w1, cmp_b1, cmp_w2, w_proj_a, w_proj_b, w_out, ln1_g, ln1_b,
           router_w, router_bias, w_gate, w_up, w_down, ws_gate, ws_up, ws_down, ln2_g, ln2_b):
    bsz, seq, d = x.shape
    depth = w_in.shape[0]
    n = bsz * seq
    alpha = float((2 * depth) ** 0.25)
    cos, sa, sb = _rope_tables(seq)
    ovt = _overlap_t(seq)
    eg = _gate_expand()
    ncb = seq // CMP_STRIDE
    ctile = 128
    x2 = x.reshape(n, d)
    for l in range(depth):
        w, b = _pack_inproj(w_in[l], b_in[l])
        (qn, qnr, kc, vc, ks, vs, kw, vw, gates, qb, kb, vb, ma, mb) = _inproj(x2, w, b, cos, sa, sb, seq)
        cw = CMP_STRIDE * NSA_KV
        kcmp, vcmp = _compress(kc.reshape(bsz, ncb, cw), vc.reshape(bsz, ncb, cw),
                               *_pack_compress(cmp_pos[l], cmp_w1[l], cmp_b1[l], cmp_w2[l]))
        ocmp, member = _nsa_cmp(qn, kcmp, vcmp, ovt, bsz, seq)
        osel = _flash("sel", qnr, ks, vs, member, bsz, seq)
        owin = _flash("win", qnr, kw, vw, member, bsz, seq)
        mmem = _moba_select(qb, _moba_kmean(kb, bsz, seq), bsz, seq)
        ob = _flash("moba", qb, kb, vb, mmem, bsz, seq)
        x1, x1s = _post(ocmp, osel, owin, ob, gates, ma, mb, x2, eg,
                        w_proj_a[l].astype(BF16), w_proj_b[l].astype(BF16), w_out[l].astype(BF16),
                        ln1_g[l][None, :], ln1_b[l][None, :], alpha)
        top_idx, gate, rank, cnt = _router(x1, router_w[l].T.astype(BF16), router_bias[l][:, None])
        counts = cnt[:, 0].astype(jnp.int32)
        dest, row_tok, row_gate, blk_expert = _dispatch_plan(top_idx, gate, rank, counts, n)
        y_slab = _experts(blk_expert, row_tok, x1s, row_gate, w_gate, w_up, w_down, l)
        dest_tiles = dest.reshape(TOP_K, n // ctile, ctile).transpose(1, 0, 2).reshape(n // ctile, TOP_K * ctile)
        x2 = _combine(dest_tiles, y_slab, x1, ws_gate[l].astype(BF16), ws_up[l].astype(BF16),
                      ws_down[l].astype(BF16), ln2_g[l][None, :], ln2_b[l][None, :], alpha, tm=ctile)
    return x2.reshape(bsz, seq, d)
```

```python
import functools

import numpy as np
import jax
import jax.numpy as jnp
from jax import lax
from jax.experimental import pallas as pl
from jax.experimental.pallas import tpu as pltpu

D_MODEL = 1024
HEAD_DIM = 64
NSA_HEADS = 8
NSA_KV_HEADS = 2
MOBA_HEADS = 8
ROPE_THETA = 10000.0
CMP_BLOCK = 32
CMP_STRIDE = 16
CMP_HIDDEN = 128
SEL_BLOCK = 64
SEL_TOPN = 16
WINDOW = 512
MOBA_BLOCK = 256
MOBA_TOPK = 3
N_EXPERTS = 256
TOP_K = 8
N_GROUPS = 8
TOPK_GROUPS = 4
D_EXPERT = 256
ROUTED_SCALE = 2.5
LN_EPS = 1e-5
NEG = -1e30
BIG = 1e30
TAKEN = -3e38

NSA_Q = NSA_HEADS * HEAD_DIM
NSA_KV = NSA_KV_HEADS * HEAD_DIM
MOBA_W = MOBA_HEADS * HEAD_DIM
IN_SPLITS = (NSA_Q, NSA_KV, NSA_KV, NSA_KV, NSA_KV, NSA_KV, NSA_KV, 3 * NSA_HEADS,
             MOBA_W, MOBA_W, MOBA_W, D_MODEL, D_MODEL)

LANES = 128
SUBLANES = 8
SLAB = D_MODEL // LANES
MOBA_SLOTS = LANES // MOBA_HEADS
ROW_BLOCK = 256
DISPATCH_TILE = 128
DMA_UNROLL = 8
FLASH_TQ = 512
ONES_ROWS = 16
LOG2E = 1.4426950408889634
VMEM_LIMIT = 52 * 1024 * 1024

F32 = jnp.float32
BF16 = jnp.bfloat16
NT = (((1,), (1,)), ((), ()))


def _cp(sem, vmem=VMEM_LIMIT):
    return pltpu.CompilerParams(dimension_semantics=sem, vmem_limit_bytes=vmem)


def _iota(shape, axis):
    return lax.broadcasted_iota(jnp.int32, shape, axis)


def _inproj_layout():
    offs = np.concatenate([[0], np.cumsum(IN_SPLITS)])
    groups = []
    pieces = []

    def src(i):
        return np.arange(offs[i], offs[i + 1])

    def dup(i):
        c = src(i)
        return np.concatenate([c[:64], c[:64], c[64:], c[64:]])

    col = 0

    def add(cols, outs):
        nonlocal col
        pieces.append(cols)
        groups.append((col, len(cols), outs))
        col += len(cols)

    add(src(0), [(False, BF16), (True, BF16)])
    add(src(1), [(False, F32)])
    add(src(2), [(False, F32)])
    add(dup(3), [(True, BF16)])
    add(dup(5), [(True, BF16)])
    add(np.concatenate([src(7), -np.ones(LANES - 3 * NSA_HEADS, np.int64)]), [(False, F32)])
    add(src(8), [(True, BF16)])
    add(src(9), [(True, BF16)])
    add(src(11), [(False, F32)])
    add(src(12), [(False, F32)])
    tgroups, tpieces, row = [], [], 0
    for cols in (dup(4), dup(6), src(10)):
        tgroups.append((row, len(cols)))
        tpieces.append(cols)
        row += len(cols)
    return groups, np.concatenate(pieces), tgroups, np.concatenate(tpieces)


_INPROJ_GROUPS, _INPROJ_COLS, _INPROJ_TGROUPS, _INPROJ_TCOLS = _inproj_layout()


def _inproj_kernel(x_ref, w_ref, b_ref, wt_ref, bt_ref, cos_ref, sa_ref, sb_ref, *out_refs):
    xb = x_ref[...].astype(BF16)
    n_tok_major = sum(len(outs) for _, _, outs in _INPROJ_GROUPS)
    for (r0, rows), o_ref in zip(_INPROJ_TGROUPS, out_refs[n_tok_major:]):
        vt = lax.dot_general(wt_ref[r0:r0 + rows, :], xb, NT, preferred_element_type=F32) + bt_ref[r0:r0 + rows, :]
        o_ref[...] = vt.astype(BF16)
    oi = 0
    for c0, width, outs in _INPROJ_GROUPS:
        acc = jnp.dot(xb, w_ref[:, c0:c0 + width], preferred_element_type=F32) + b_ref[:, c0:c0 + width]
        for rope, dtype in outs:
            o_ref = out_refs[oi]
            oi += 1
            if not rope:
                o_ref[...] = acc.astype(dtype)
                continue
            cos = cos_ref[...]
            sa = sa_ref[...]
            sb = sb_ref[...]
            for j in range(0, width, LANES):
                t = acc[:, j:j + LANES]
                r = t * cos + pltpu.roll(t, LANES - HEAD_DIM // 2, 1) * sa + pltpu.roll(t, HEAD_DIM // 2, 1) * sb
                o_ref[:, j:j + LANES] = r.astype(dtype)


def _inproj(x2, w, b, wt, bt, cos, sa, sb, seq, tm=256):
    n = x2.shape[0]
    ctot = w.shape[1]
    rtot = wt.shape[0]
    nt_seq = seq // tm
    out_shape, out_specs = [], []
    for _, width, outs in _INPROJ_GROUPS:
        for _, dtype in outs:
            out_shape.append(jax.ShapeDtypeStruct((n, width), dtype))
            out_specs.append(pl.BlockSpec((tm, width), lambda i: (i, 0)))
    for _, rows in _INPROJ_TGROUPS:
        out_shape.append(jax.ShapeDtypeStruct((rows, n), BF16))
        out_specs.append(pl.BlockSpec((rows, tm), lambda i: (0, i)))
    tab = pl.BlockSpec((tm, LANES), lambda i: (i % nt_seq, 0))
    return pl.pallas_call(
        _inproj_kernel,
        grid=(n // tm,),
        in_specs=[pl.BlockSpec((tm, D_MODEL), lambda i: (i, 0)),
                  pl.BlockSpec((D_MODEL, ctot), lambda i: (0, 0)),
                  pl.BlockSpec((1, ctot), lambda i: (0, 0)),
                  pl.BlockSpec((rtot, D_MODEL), lambda i: (0, 0)),
                  pl.BlockSpec((rtot, 1), lambda i: (0, 0)),
                  tab, tab, tab],
        out_specs=out_specs,
        out_shape=out_shape,
        compiler_params=_cp(("parallel",)),
        name="inproj",
    )(x2, w, b, wt, bt, cos, sa, sb)


def _compress_kernel(kc_ref, vc_ref, ptop_ref, pbot_ref, wtop_ref, wbot_ref, b1_ref, w2_ref, ko_ref, vo_ref):
    ncb = kc_ref.shape[1]
    for kv, (c_ref, o_ref) in enumerate(((kc_ref, ko_ref), (vc_ref, vo_ref))):
        c = c_ref[0]
        a = jnp.dot((c + ptop_ref[kv]).astype(BF16), wtop_ref[kv], preferred_element_type=F32)
        bm = jnp.dot((c + pbot_ref[kv]).astype(BF16), wbot_ref[kv], preferred_element_type=F32)
        hid = jax.nn.gelu(a + pltpu.roll(bm, ncb - 1, 0) + b1_ref[kv])
        for g in range(NSA_KV_HEADS):
            hg = hid[:, g * CMP_HIDDEN:(g + 1) * CMP_HIDDEN].astype(BF16)
            o_ref[0, g] = jnp.dot(hg, w2_ref[kv], preferred_element_type=F32).astype(BF16)


def _compress(kc3, vc3, ptop, pbot, wtop, wbot, b1, w2):
    bsz, ncb, cw = kc3.shape
    hw = NSA_KV_HEADS * CMP_HIDDEN
    full = lambda shp: pl.BlockSpec(shp, lambda b: (0,) * len(shp))
    out = jax.ShapeDtypeStruct((bsz, NSA_KV_HEADS, ncb, LANES), BF16)
    ospec = pl.BlockSpec((1, NSA_KV_HEADS, ncb, LANES), lambda b: (b, 0, 0, 0))
    return pl.pallas_call(
        _compress_kernel,
        grid=(bsz,),
        in_specs=[pl.BlockSpec((1, ncb, cw), lambda b: (b, 0, 0)),
                  pl.BlockSpec((1, ncb, cw), lambda b: (b, 0, 0)),
                  full((2, 1, cw)), full((2, 1, cw)),
                  full((2, cw, hw)), full((2, cw, hw)),
                  full((2, 1, hw)), full((2, CMP_HIDDEN, LANES))],
        out_specs=[ospec, ospec],
        out_shape=[out, out],
        compiler_params=_cp(("parallel",)),
        name="nsa_compress",
    )(kc3, vc3, ptop, pbot, wtop, wbot, b1, w2)


def _rank_rows(val, n_rows):
    rows = _iota(val.shape, 0)
    rank = jnp.zeros(val.shape, F32)
    for i in range(n_rows):
        vi = val[i:i + 1, :]
        ahead = (vi > val) | ((vi == val) & (rows > i))
        rank = rank + jnp.where(ahead, 1.0, 0.0)
    return rank


def _transpose_to_rows(mt):
    tq = mt.shape[1]
    return jnp.concatenate([mt[:, c:c + LANES].T for c in range(0, tq, LANES)], axis=0)


def _nsa_cmp_kernel(q_ref, kc_ref, vc_ref, ov_ref, o_ref, mem_ref, *, tq, n_cmp, n_sel):
    i = pl.program_id(2)
    ncb = kc_ref.shape[2]
    kc = kc_ref[0, 0]
    vc = vc_ref[0, 0]
    pos = i * tq + _iota((tq, ncb), 0)
    blk = _iota((tq, ncb), 1)
    valid = (blk * CMP_STRIDE + CMP_BLOCK - 1 <= pos) & (blk < n_cmp)
    anyv = (i * tq + _iota((tq, 1), 0) >= CMP_BLOCK - 1).astype(F32)
    lane = _iota((1, LANES), 1)
    scale = HEAD_DIM ** -0.5
    psum = jnp.zeros((tq, ncb), F32)
    outs = []
    hg = NSA_HEADS // NSA_KV_HEADS
    for t in range(hg):
        qb = q_ref[:, (t // 2) * LANES:(t // 2 + 1) * LANES]
        half = (lane < HEAD_DIM) if t % 2 == 0 else (lane >= HEAD_DIM)
        qm = jnp.where(half, qb, jnp.zeros_like(qb))
        s = lax.dot_general(qm, kc, NT, preferred_element_type=F32) * scale
        s = jnp.where(valid, s, NEG)
        e = jnp.exp(s - jnp.max(s, axis=1, keepdims=True))
        p = e / jnp.sum(e, axis=1, keepdims=True) * anyv
        psum = psum + p
        outs.append(jnp.dot(p.astype(BF16), vc, preferred_element_type=F32))
    for r in range(hg // 2):
        o_ref[:, r * LANES:(r + 1) * LANES] = jnp.where(lane < HEAD_DIM, outs[2 * r], outs[2 * r + 1]).astype(BF16)

    imp = lax.dot_general(ov_ref[...], psum, NT, preferred_element_type=F32, precision=lax.Precision.HIGHEST)
    imp = imp[:n_sel]
    j = _iota((n_sel, tq), 0)
    cur = (i * tq + _iota((n_sel, tq), 1)) // SEL_BLOCK
    forced = (j == 0) | (j == cur) | (j == cur - 1)
    val = jnp.where(forced, BIG, jnp.where(j <= cur, imp, NEG))
    rank = _rank_rows(val, n_sel)
    member = jnp.where((rank < float(min(SEL_TOPN, n_sel))) & (j <= cur), 1.0, 0.0)
    if n_sel < LANES:
        member = jnp.concatenate([member, jnp.zeros((LANES - n_sel, tq), F32)], axis=0)
    mem_ref[0, 0] = member


def _nsa_cmp(qn, kcmp, vcmp, ovt, bsz, seq, tq=256):
    n = qn.shape[0]
    nq = seq // tq
    ncb = kcmp.shape[2]
    n_sel = seq // SEL_BLOCK
    gw = NSA_Q // NSA_KV_HEADS
    kern = functools.partial(_nsa_cmp_kernel, tq=tq, n_cmp=ncb - 1, n_sel=n_sel)
    kvspec = pl.BlockSpec((1, 1, ncb, LANES), lambda b, g, i: (b, g, 0, 0))
    return pl.pallas_call(
        kern,
        grid=(bsz, NSA_KV_HEADS, nq),
        in_specs=[pl.BlockSpec((tq, gw), lambda b, g, i: (b * nq + i, g)),
                  kvspec, kvspec,
                  pl.BlockSpec((LANES, ncb), lambda b, g, i: (0, 0))],
        out_specs=[pl.BlockSpec((tq, gw), lambda b, g, i: (b * nq + i, g)),
                   pl.BlockSpec((1, 1, LANES, tq), lambda b, g, i: (b, g, 0, i))],
        out_shape=[jax.ShapeDtypeStruct((n, NSA_Q), BF16),
                   jax.ShapeDtypeStruct((bsz, NSA_KV_HEADS, LANES, seq), F32)],
        compiler_params=_cp(("parallel", "parallel", "parallel")),
        name="nsa_cmp_select",
    )(qn, kcmp, vcmp, ovt)


def _moba_kmean_kernel(k_ref, o_ref):
    nblk = o_ref.shape[1]
    k = k_ref[...].astype(F32).reshape(nblk, MOBA_BLOCK, MOBA_W)
    o_ref[0] = jnp.sum(k, axis=1) * (1.0 / MOBA_BLOCK)


def _moba_kmean(kb, bsz, seq):
    nblk = seq // MOBA_BLOCK
    return pl.pallas_call(
        _moba_kmean_kernel,
        grid=(bsz,),
        in_specs=[pl.BlockSpec((seq, MOBA_W), lambda b: (b, 0))],
        out_specs=pl.BlockSpec((1, nblk, MOBA_W), lambda b: (b, 0, 0)),
        out_shape=jax.ShapeDtypeStruct((bsz, nblk, MOBA_W), F32),
        compiler_params=_cp(("parallel",)),
        name="moba_kmean",
    )(kb)


def _moba_select_kernel(q_ref, km_ref, mem_ref, *, tq, nblk):
    i = pl.program_id(1)
    km = km_ref[0]
    if nblk < MOBA_SLOTS:
        km = jnp.concatenate([km, jnp.zeros((MOBA_SLOTS - nblk, MOBA_W), F32)], axis=0)
    kmt = jnp.concatenate([km] * MOBA_HEADS, axis=0)
    r = _iota((LANES, MOBA_W), 0)
    c = _iota((LANES, MOBA_W), 1)
    kmt = jnp.where(r // MOBA_SLOTS == c // HEAD_DIM, kmt, 0.0).astype(BF16)
    gate = lax.dot_general(kmt, q_ref[...], NT, preferred_element_type=F32)
    members = []
    for h in range(MOBA_HEADS):
        sub = gate[h * MOBA_SLOTS:(h + 1) * MOBA_SLOTS]
        j = _iota((MOBA_SLOTS, tq), 0)
        cur = (i * tq + _iota((MOBA_SLOTS, tq), 1)) // MOBA_BLOCK
        past = j < cur
        val = jnp.where(past, sub, NEG)
        rank = _rank_rows(val, min(nblk, MOBA_SLOTS))
        members.append(jnp.where(((rank < float(MOBA_TOPK)) & past) | (j == cur), 1.0, 0.0))
    mem_ref[...] = jnp.concatenate(members, axis=0)


def _moba_select(qb, kmean, bsz, seq, tq=256):
    n = qb.shape[0]
    nq = seq // tq
    nblk = seq // MOBA_BLOCK
    kern = functools.partial(_moba_select_kernel, tq=tq, nblk=nblk)
    return pl.pallas_call(
        kern,
        grid=(bsz, nq),
        in_specs=[pl.BlockSpec((tq, MOBA_W), lambda b, i: (b * nq + i, 0)),
                  pl.BlockSpec((1, nblk, MOBA_W), lambda b, i: (b, 0, 0))],
        out_specs=pl.BlockSpec((LANES, tq), lambda b, i: (0, b * nq + i)),
        out_shape=jax.ShapeDtypeStruct((LANES, n), F32),
        compiler_params=_cp(("parallel", "parallel")),
        name="moba_select",
    )(qb, kmean)


def _flash_kernel(qi_ref, kj_ref, first_ref, last_ref, edge_ref, q_ref, k_ref, vt_ref, mem_ref, o_ref,
                  m_sc, acc_sc, *, mode, nhq, tq, tk):
    hg = pl.program_id(1)
    p = pl.program_id(2)
    i = qi_ref[p]
    jj = kj_ref[p]

    @pl.when(first_ref[p] == 1)
    def _():
        m_sc[...] = jnp.full(m_sc.shape, NEG, F32)
        acc_sc[...] = jnp.zeros(acc_sc.shape, F32)

    def member_rows(first_row, keys_per_row):
        rows = [mem_ref[pl.ds(first_row + r, 1), :] > 0.5 for r in range(tk // keys_per_row)]
        return jnp.concatenate([jnp.broadcast_to(r, (keys_per_row, tq)) for r in rows], axis=0)

    def step(positional):
        allowed = None
        if positional:
            kpos = jj * tk + _iota((tk, tq), 0)
            qpos = i * tq + _iota((tk, tq), 1)
            allowed = kpos <= qpos
            if mode == "win":
                allowed = allowed & (kpos > qpos - WINDOW)
        if mode == "sel":
            memb = member_rows(jj * (tk // SEL_BLOCK), SEL_BLOCK)
            allowed = memb if allowed is None else allowed & memb
        k = k_ref[...]
        vt1 = jnp.concatenate([vt_ref[...], jnp.ones((ONES_ROWS, tk), BF16)], axis=0)
        lane = _iota((1, LANES), 1)
        qms = []
        for t in range(nhq):
            qb = q_ref[:, (t // 2) * LANES:(t // 2 + 1) * LANES].astype(F32)
            half = (lane < HEAD_DIM) if t % 2 == 0 else (lane >= HEAD_DIM)
            qms.append(jnp.where(half, qb * (HEAD_DIM ** -0.5 * LOG2E), 0.0).astype(BF16))
        s_all = lax.dot_general(k, jnp.concatenate(qms, axis=0), NT, preferred_element_type=F32)
        prs, alphas = [], []
        for t in range(nhq):
            s = s_all[:, t * tq:(t + 1) * tq]
            ok = allowed
            if mode == "moba":
                memb = member_rows((hg * 2 + t) * MOBA_SLOTS + jj * (tk // MOBA_BLOCK), MOBA_BLOCK)
                ok = memb if ok is None else ok & memb
            if ok is not None:
                s = jnp.where(ok, s, NEG)
            m_old = m_sc[:, t * tq:(t + 1) * tq]
            m_new = jnp.maximum(m_old, jnp.max(s, axis=0, keepdims=True))
            prs.append(jnp.exp2(s - m_new).astype(BF16))
            alphas.append(jnp.exp2(m_old - m_new))
            m_sc[:, t * tq:(t + 1) * tq] = m_new
        pv = jnp.dot(vt1, jnp.concatenate(prs, axis=1), preferred_element_type=F32)
        acc_sc[...] = jnp.concatenate(alphas, axis=1) * acc_sc[...] + pv

    @pl.when(edge_ref[p] == 1)
    def _():
        step(True)

    @pl.when(edge_ref[p] == 0)
    def _():
        step(False)

    @pl.when(last_ref[p] == 1)
    def _():
        for r in range(nhq // 2):
            a0 = acc_sc[:, 2 * r * tq:(2 * r + 1) * tq]
            a1 = acc_sc[:, (2 * r + 1) * tq:(2 * r + 2) * tq]
            o0 = a0[:HEAD_DIM] / a0[LANES:LANES + 1]
            o1 = a1[HEAD_DIM:LANES] / a1[LANES:LANES + 1]
            ot = jnp.concatenate([o0, o1], axis=0)
            o_ref[:, r * LANES:(r + 1) * LANES] = _transpose_to_rows(ot).astype(BF16)


def _pair_tables(nq, tq, tk, window):
    qi, kj, first, last, edge = [], [], [], [], []
    for i in range(nq):
        q_lo, q_hi = i * tq, i * tq + tq - 1
        lo = 0 if window is None else max(0, (q_lo - window + 1) // tk)
        hi = q_hi // tk
        for j in range(lo, hi + 1):
            k_lo, k_hi = j * tk, j * tk + tk - 1
            inside = k_hi <= q_lo and (window is None or k_lo > q_hi - window)
            qi.append(i)
            kj.append(j)
            first.append(int(j == lo))
            last.append(int(j == hi))
            edge.append(int(not inside))
    mk = lambda a: jnp.asarray(np.asarray(a, np.int32))
    return mk(qi), mk(kj), mk(first), mk(last), mk(edge)


def _flash(mode, q, k, vt, mem, bsz, seq, tq=FLASH_TQ, tk=MOBA_BLOCK):
    assert tk % MOBA_BLOCK == 0 and tk % SEL_BLOCK == 0 and seq % tq == 0 and seq % tk == 0
    n = q.shape[0]
    nq = seq // tq
    nk = seq // tk
    if mode == "moba":
        ngrp, nhq = MOBA_HEADS // 2, 2
        mem_spec = pl.BlockSpec((LANES, tq), lambda b, g, p, qi, kj, fi, la, ed: (0, b * nq + qi[p]))
    else:
        ngrp, nhq = NSA_KV_HEADS, NSA_HEADS // NSA_KV_HEADS
        mem_spec = pl.BlockSpec((None, None, LANES, tq), lambda b, g, p, qi, kj, fi, la, ed: (b, g, 0, qi[p]))
    tables = _pair_tables(nq, tq, tk, WINDOW if mode == "win" else None)
    npairs = int(tables[0].shape[0])
    qw = nhq // 2 * LANES
    kern = functools.partial(_flash_kernel, mode=mode, nhq=nhq, tq=tq, tk=tk)
    kspec = pl.BlockSpec((tk, LANES), lambda b, g, p, qi, kj, fi, la, ed: (b * nk + kj[p], g))
    vtspec = pl.BlockSpec((LANES, tk), lambda b, g, p, qi, kj, fi, la, ed: (g, b * nk + kj[p]))
    qspec = pl.BlockSpec((tq, qw), lambda b, g, p, qi, kj, fi, la, ed: (b * nq + qi[p], g))
    grid_spec = pltpu.PrefetchScalarGridSpec(
        num_scalar_prefetch=5,
        grid=(bsz, ngrp, npairs),
        in_specs=[qspec, kspec, vtspec, mem_spec],
        out_specs=qspec,
        scratch_shapes=[pltpu.VMEM((1, nhq * tq), F32),
                        pltpu.VMEM((LANES + ONES_ROWS, nhq * tq), F32)],
    )
    return pl.pallas_call(
        kern,
        grid_spec=grid_spec,
        out_shape=jax.ShapeDtypeStruct((n, ngrp * qw), BF16),
        compiler_params=_cp(("parallel", "parallel", "arbitrary")),
        name="flash_" + mode,
    )(*tables, q, k, vt, mem)


def _layer_norm(z, g, b):
    mu = jnp.mean(z, axis=-1, keepdims=True)
    zc = z - mu
    var = jnp.mean(zc * zc, axis=-1, keepdims=True)
    return zc * lax.rsqrt(var + LN_EPS) * g + b


def _store_slabs(slab_ref, val, rows):
    for c in range(SLAB):
        slab_ref[pl.ds(c, rows, stride=SLAB), :] = val[:, c * LANES:(c + 1) * LANES]


def _post_kernel(ocmp_ref, osel_ref, owin_ref, ob_ref, g_ref, ma_ref, mb_ref, x_ref,
                 eg_ref, wpa_ref, wpb_ref, wout_ref, lng_ref, lnb_ref, x1_ref, x1s_ref, *, alpha, tm):
    sig = jax.nn.sigmoid(g_ref[...])
    oa = jnp.zeros((tm, NSA_Q), F32)
    for c, o_ref in enumerate((ocmp_ref, osel_ref, owin_ref)):
        gexp = jnp.dot(sig, eg_ref[c], preferred_element_type=F32, precision=lax.Precision.HIGHEST)
        oa = oa + gexp * o_ref[...].astype(F32)
    pa = jnp.dot(oa.astype(BF16), wpa_ref[...], preferred_element_type=F32)
    pb = jnp.dot(ob_ref[...], wpb_ref[...], preferred_element_type=F32)
    merged = jax.nn.sigmoid(ma_ref[...]) * pa + jax.nn.sigmoid(mb_ref[...]) * pb
    y = jnp.dot(merged.astype(BF16), wout_ref[...], preferred_element_type=F32)
    x1 = _layer_norm(alpha * x_ref[...] + y, lng_ref[...], lnb_ref[...])
    x1_ref[...] = x1
    _store_slabs(x1s_ref, x1, tm)


def _post(ocmp, osel, owin, ob, gates, ma, mb, x2, eg, wpa, wpb, wout, lng, lnb, alpha, tm=256):
    n = x2.shape[0]
    row = lambda w: pl.BlockSpec((tm, w), lambda i: (i, 0))
    full = lambda shp: pl.BlockSpec(shp, lambda i: (0,) * len(shp))
    kern = functools.partial(_post_kernel, alpha=alpha, tm=tm)
    return pl.pallas_call(
        kern,
        grid=(n // tm,),
        in_specs=[row(NSA_Q), row(NSA_Q), row(NSA_Q), row(MOBA_W), row(LANES), row(D_MODEL), row(D_MODEL),
                  row(D_MODEL), full((3, LANES, NSA_Q)), full((NSA_Q, D_MODEL)), full((MOBA_W, D_MODEL)),
                  full((D_MODEL, D_MODEL)), full((1, D_MODEL)), full((1, D_MODEL))],
        out_specs=[row(D_MODEL), pl.BlockSpec((tm * SLAB, LANES), lambda i: (i, 0))],
        out_shape=[jax.ShapeDtypeStruct((n, D_MODEL), F32), jax.ShapeDtypeStruct((n * SLAB, LANES), F32)],
        compiler_params=_cp(("parallel",)),
        name="post_attention",
    )(ocmp, osel, owin, ob, gates, ma, mb, x2, eg, wpa, wpb, wout, lng, lnb)


def _first_argmax_rows(cur, rows, n_rows):
    mx = jnp.max(cur, axis=0, keepdims=True)
    idx = jnp.min(jnp.where(cur == mx, rows, n_rows), axis=0, keepdims=True)
    return mx, idx


def _router_kernel(x_ref, rwt_ref, bias_ref, idx_ref, gate_ref, rank_ref, cnt_ref, carry_sc, *, tm):
    i = pl.program_id(0)

    @pl.when(i == 0)
    def _():
        carry_sc[...] = jnp.zeros(carry_sc.shape, F32)

    logits = lax.dot_general(rwt_ref[...], x_ref[...].astype(BF16), NT, preferred_element_type=F32)
    sc = jax.nn.sigmoid(logits)
    biased = sc + bias_ref[...]
    gsz = N_EXPERTS // N_GROUPS
    grow = _iota((gsz, tm), 0)
    gscore = []
    for g in range(N_GROUPS):
        grp = biased[g * gsz:(g + 1) * gsz]
        m1, i1 = _first_argmax_rows(grp, grow, gsz)
        m2 = jnp.max(jnp.where(grow == i1, TAKEN, grp), axis=0, keepdims=True)
        gscore.append(m1 + m2)
    gs = jnp.concatenate(gscore, axis=0)
    keep = _rank_rows(gs, N_GROUPS) < float(TOPK_GROUPS)
    cur = jnp.concatenate(
        [jnp.where(keep[g:g + 1], biased[g * gsz:(g + 1) * gsz], NEG) for g in range(N_GROUPS)], axis=0)
    rows = _iota((N_EXPERTS, tm), 0)
    sels, tops, idxs = [], [], []
    for _ in range(TOP_K):
        _, ik = _first_argmax_rows(cur, rows, N_EXPERTS)
        selk = rows == ik
        tops.append(jnp.sum(jnp.where(selk, sc, 0.0), axis=0, keepdims=True))
        cur = jnp.where(selk, TAKEN, cur)
        sels.append(selk)
        idxs.append(ik)
    denom = tops[0]
    for tk_ in tops[1:]:
        denom = denom + tk_
    selall = jnp.where(cur == TAKEN, 1.0, 0.0)
    upper = jnp.where(_iota((tm, tm), 0) < _iota((tm, tm), 1), 1.0, 0.0).astype(BF16)
    before = jnp.dot(selall.astype(BF16), upper, preferred_element_type=F32) + carry_sc[...]
    idx_ref[...] = jnp.concatenate(idxs, axis=0)
    gate_ref[...] = jnp.concatenate([tk_ / denom * ROUTED_SCALE for tk_ in tops], axis=0)
    rank_ref[...] = jnp.concatenate(
        [jnp.sum(jnp.where(s, before, 0.0), axis=0, keepdims=True) for s in sels], axis=0).astype(jnp.int32)
    carry_sc[...] = carry_sc[...] + jnp.sum(selall, axis=1, keepdims=True)
    cnt_ref[...] = jnp.broadcast_to(carry_sc[...], cnt_ref.shape)


def _router(x1, rwt, bias, tm=256):
    n = x1.shape[0]
    kspec = pl.BlockSpec((TOP_K, tm), lambda i: (0, i))
    return pl.pallas_call(
        functools.partial(_router_kernel, tm=tm),
        grid=(n // tm,),
        in_specs=[pl.BlockSpec((tm, D_MODEL), lambda i: (i, 0)),
                  pl.BlockSpec((N_EXPERTS, D_MODEL), lambda i: (0, 0)),
                  pl.BlockSpec((N_EXPERTS, 1), lambda i: (0, 0))],
        out_specs=[kspec, kspec, kspec, pl.BlockSpec((N_EXPERTS, LANES), lambda i: (0, 0))],
        out_shape=[jax.ShapeDtypeStruct((TOP_K, n), jnp.int32), jax.ShapeDtypeStruct((TOP_K, n), F32),
                   jax.ShapeDtypeStruct((TOP_K, n), jnp.int32), jax.ShapeDtypeStruct((N_EXPERTS, LANES), F32)],
        scratch_shapes=[pltpu.VMEM((N_EXPERTS, 1), F32)],
        compiler_params=_cp(("arbitrary",)),
        name="router",
    )(x1, rwt, bias)


def _dest_kernel(idx_ref, rank_ref, pstart_ref, o_ref, *, tm):
    rows = _iota((N_EXPERTS, tm), 0)
    pstart = pstart_ref[...]
    for k in range(TOP_K):
        hit = rows == idx_ref[k:k + 1, :]
        base = jnp.sum(jnp.where(hit, pstart, 0.0), axis=0, keepdims=True)
        o_ref[:, k * tm:(k + 1) * tm] = base.astype(jnp.int32) + rank_ref[k:k + 1, :]


def _dest_rows(top_idx, rank, pad_start, tm=DISPATCH_TILE):
    n = top_idx.shape[1]
    kspec = pl.BlockSpec((TOP_K, tm), lambda i: (0, i))
    out = pl.pallas_call(
        functools.partial(_dest_kernel, tm=tm),
        grid=(n // tm,),
        in_specs=[kspec, kspec, pl.BlockSpec((N_EXPERTS, 1), lambda i: (0, 0))],
        out_specs=pl.BlockSpec((None, 1, TOP_K * tm), lambda i: (i, 0, 0)),
        out_shape=jax.ShapeDtypeStruct((n // tm, 1, TOP_K * tm), jnp.int32),
        compiler_params=_cp(("parallel",)),
        name="moe_dest_rows",
    )(top_idx, rank, pad_start.astype(F32)[:, None])
    return out.reshape(n // tm, TOP_K * tm)


def _issue_unrolled(count, issue):
    def body(c, carry):
        for u in range(DMA_UNROLL):
            issue(c * DMA_UNROLL + u)
        return carry
    lax.fori_loop(0, count // DMA_UNROLL, body, 0)


def _dispatch_kernel(cnt_ref, pstart_ref, nused_ref, dest_hbm, x_hbm, xs_hbm, idx_sm, zero_sc, idx_sem, row_sem,
                     zero_sem, *, ntile, tm, nblk):
    i = pl.program_id(0)
    slot = i % 2
    nrow = TOP_K * tm

    def idx_copy(t, s):
        return pltpu.make_async_copy(dest_hbm.at[t], idx_sm.at[s], idx_sem.at[s])

    def tile_wait(s):
        pltpu.make_async_copy(x_hbm.at[pl.ds(0, nrow * SLAB)], xs_hbm.at[pl.ds(0, nrow * SLAB)], row_sem.at[s]).wait()

    def zero_copy(row, rows):
        return pltpu.make_async_copy(zero_sc.at[pl.ds(0, rows * SLAB)],
                                     xs_hbm.at[pl.ds(pl.multiple_of(row * SLAB, SLAB), rows * SLAB)], zero_sem.at[0])

    def zero_fill(act):
        def per_expert(e, carry):
            cnt = cnt_ref[e]
            npad = (ROW_BLOCK - cnt % ROW_BLOCK) % ROW_BLOCK
            row = pstart_ref[e] + cnt
            piece = ROW_BLOCK // 2
            while piece >= 1:
                @pl.when((npad & piece) != 0)
                def _(row=row, piece=piece):
                    act(zero_copy(row, piece))
                row = row + (npad & piece)
                piece //= 2
            return carry
        lax.fori_loop(0, N_EXPERTS, per_expert, 0)

        def per_block(bk, carry):
            act(zero_copy(bk * ROW_BLOCK, ROW_BLOCK))
            return carry
        lax.fori_loop(nused_ref[0], nblk, per_block, 0)

    @pl.when(i == 0)
    def _():
        idx_copy(0, 0).start()
        if ntile > 1:
            idx_copy(1, 1).start()
        zero_sc[...] = jnp.zeros(zero_sc.shape, F32)
        zero_fill(lambda cp: cp.start())
        zero_fill(lambda cp: cp.wait())

    idx_copy(i, slot).wait()
    tok0 = i * tm

    def issue(r):
        src = x_hbm.at[pl.ds(pl.multiple_of((tok0 + r % tm) * SLAB, SLAB), SLAB)]
        dst = xs_hbm.at[pl.ds(pl.multiple_of(idx_sm[slot, r] * SLAB, SLAB), SLAB)]
        pltpu.make_async_copy(src, dst, row_sem.at[slot]).start()
    _issue_unrolled(nrow, issue)

    @pl.when(i >= 1)
    def _():
        tile_wait(1 - slot)

    @pl.when(i + 2 < ntile)
    def _():
        idx_copy(i + 2, slot).start()

    @pl.when(i == ntile - 1)
    def _():
        tile_wait(slot)


def _dispatch(counts, pad_start, n_used, dest_tiles, x1s, nblk, tm=DISPATCH_TILE):
    ntile = dest_tiles.shape[0]
    grid_spec = pltpu.PrefetchScalarGridSpec(
        num_scalar_prefetch=3,
        grid=(ntile,),
        in_specs=[pl.BlockSpec(memory_space=pl.ANY), pl.BlockSpec(memory_space=pl.ANY)],
        out_specs=pl.BlockSpec(memory_space=pl.ANY),
        scratch_shapes=[pltpu.SMEM((2, TOP_K * tm), jnp.int32),
                        pltpu.VMEM((ROW_BLOCK * SLAB, LANES), F32),
                        pltpu.SemaphoreType.DMA((2,)),
                        pltpu.SemaphoreType.DMA((2,)),
                        pltpu.SemaphoreType.DMA((1,))],
    )
    return pl.pallas_call(
        functools.partial(_dispatch_kernel, ntile=ntile, tm=tm, nblk=nblk),
        grid_spec=grid_spec,
        out_shape=jax.ShapeDtypeStruct((nblk * ROW_BLOCK * SLAB, LANES), F32),
        compiler_params=_cp(("arbitrary",)),
        name="moe_dispatch",
    )(counts, pad_start, n_used, dest_tiles, x1s)


def _expert_kernel(be_ref, nused_ref, xs_ref, wg_ref, wu_ref, wd_ref, y_ref, xmat, wgu_sc, wd_sc):
    i = pl.program_id(0)
    rb = ROW_BLOCK

    @pl.when(i < nused_ref[0])
    def _():
        e = be_ref[i]
        prev = be_ref[jnp.maximum(i - 1, 0)]

        @pl.when((i == 0) | (e != prev))
        def _():
            wgu_sc[:, :D_EXPERT] = wg_ref[0, 0].astype(BF16)
            wgu_sc[:, D_EXPERT:] = wu_ref[0, 0].astype(BF16)
            wd_sc[...] = wd_ref[0, 0].astype(BF16)

        for c in range(SLAB):
            xmat[:, c * LANES:(c + 1) * LANES] = xs_ref[pl.ds(c, rb, stride=SLAB), :].astype(BF16)
        hgu = jnp.dot(xmat[...], wgu_sc[...], preferred_element_type=F32)
        h = jax.nn.silu(hgu[:, :D_EXPERT]) * hgu[:, D_EXPERT:]
        y = jnp.dot(h.astype(BF16), wd_sc[...], preferred_element_type=F32)
        _store_slabs(y_ref, y, rb)

    @pl.when(i >= nused_ref[0])
    def _():
        y_ref[...] = jnp.zeros(y_ref.shape, F32)


def _experts(blk_expert, n_used, xs, w_gate, w_up, w_down, layer):
    nblk = blk_expert.shape[0]
    rb = ROW_BLOCK
    blk = lambda i, nu: jnp.minimum(i, nu[0] - 1)
    wspec = lambda shp: pl.BlockSpec((1, 1) + shp, lambda i, be, nu: (layer, be[blk(i, nu)], 0, 0))
    grid_spec = pltpu.PrefetchScalarGridSpec(
        num_scalar_prefetch=2,
        grid=(nblk,),
        in_specs=[pl.BlockSpec((rb * SLAB, LANES), lambda i, be, nu: (blk(i, nu), 0)),
                  wspec((D_MODEL, D_EXPERT)), wspec((D_MODEL, D_EXPERT)), wspec((D_EXPERT, D_MODEL))],
        out_specs=pl.BlockSpec((rb * SLAB, LANES), lambda i, be, nu: (i, 0)),
        scratch_shapes=[pltpu.VMEM((rb, D_MODEL), BF16),
                        pltpu.VMEM((D_MODEL, 2 * D_EXPERT), BF16),
                        pltpu.VMEM((D_EXPERT, D_MODEL), BF16)],
    )
    return pl.pallas_call(
        _expert_kernel,
        grid_spec=grid_spec,
        out_shape=jax.ShapeDtypeStruct((nblk * rb * SLAB, LANES), F32),
        compiler_params=_cp(("arbitrary",)),
        name="moe_experts",
    )(blk_expert, n_used, xs, w_gate, w_up, w_down)


def _combine_kernel(dest_hbm, y_hbm, gate_ref, x1_ref, wsg_ref, wsu_ref, wsd_ref, lng_ref, lnb_ref, o_ref,
                    idx_sm, ybuf, gbuf, acc_sc, idx_sem, row_sem, *, ntile, tm, alpha):
    i = pl.program_id(0)
    slot = i % 2
    nrow = TOP_K * tm

    def idx_copy(t, s):
        return pltpu.make_async_copy(dest_hbm.at[t], idx_sm.at[s], idx_sem.at[s])

    def gather(s):
        def issue(r):
            src = y_hbm.at[pl.ds(pl.multiple_of(idx_sm[s, r] * SLAB, SLAB), SLAB)]
            pltpu.make_async_copy(src, ybuf.at[s, pl.ds(r * SLAB, SLAB)], row_sem.at[s]).start()
        _issue_unrolled(nrow, issue)

    @pl.when(i == 0)
    def _():
        idx_copy(0, 0).start()
        idx_copy(0, 0).wait()
        gather(0)
        if ntile > 1:
            idx_copy(1, 1).start()

    @pl.when(i + 1 < ntile)
    def _():
        idx_copy(i + 1, 1 - slot).wait()
        gather(1 - slot)

    @pl.when(i + 2 < ntile)
    def _():
        idx_copy(i + 2, slot).start()

    x1 = x1_ref[...]
    xb = x1.astype(BF16)
    hs = jax.nn.silu(jnp.dot(xb, wsg_ref[...], preferred_element_type=F32)) * jnp.dot(
        xb, wsu_ref[...], preferred_element_type=F32)
    shared = jnp.dot(hs.astype(BF16), wsd_ref[...], preferred_element_type=F32)

    for k in range(TOP_K):
        gcol = _transpose_to_rows(jnp.broadcast_to(gate_ref[k:k + 1, :], (LANES, tm)))
        for c in range(SLAB):
            gbuf[pl.ds(k * tm * SLAB + c, tm, stride=SLAB), :] = gcol

    pltpu.make_async_copy(y_hbm.at[pl.ds(0, nrow * SLAB)], ybuf.at[slot], row_sem.at[slot]).wait()
    acc = ybuf[slot, pl.ds(0, tm * SLAB), :] * gbuf[pl.ds(0, tm * SLAB), :]
    for k in range(1, TOP_K):
        acc = acc + ybuf[slot, pl.ds(k * tm * SLAB, tm * SLAB), :] * gbuf[pl.ds(k * tm * SLAB, tm * SLAB), :]
    acc_sc[...] = acc
    routed = jnp.concatenate([acc_sc[pl.ds(c, tm, stride=SLAB), :] for c in range(SLAB)], axis=1)
    o_ref[...] = _layer_norm(alpha * x1 + routed + shared, lng_ref[...], lnb_ref[...])


def _combine(dest_tiles, y_slab, gate, x1, wsg, wsu, wsd, lng, lnb, alpha, tm=DISPATCH_TILE):
    n = x1.shape[0]
    ntile = n // tm
    nrow = TOP_K * tm
    full = lambda shp: pl.BlockSpec(shp, lambda i: (0,) * len(shp))
    kern = functools.partial(_combine_kernel, ntile=ntile, tm=tm, alpha=alpha)
    return pl.pallas_call(
        kern,
        grid=(ntile,),
        in_specs=[pl.BlockSpec(memory_space=pl.ANY),
                  pl.BlockSpec(memory_space=pl.ANY),
                  pl.BlockSpec((TOP_K, tm), lambda i: (0, i)),
                  pl.BlockSpec((tm, D_MODEL), lambda i: (i, 0)),
                  full((D_MODEL, D_EXPERT)), full((D_MODEL, D_EXPERT)), full((D_EXPERT, D_MODEL)),
                  full((1, D_MODEL)), full((1, D_MODEL))],
        out_specs=pl.BlockSpec((tm, D_MODEL), lambda i: (i, 0)),
        out_shape=jax.ShapeDtypeStruct((n, D_MODEL), F32),
        scratch_shapes=[pltpu.SMEM((2, nrow), jnp.int32),
                        pltpu.VMEM((2, nrow * SLAB, LANES), F32),
                        pltpu.VMEM((nrow * SLAB, LANES), F32),
                        pltpu.VMEM((tm * SLAB, LANES), F32),
                        pltpu.SemaphoreType.DMA((2,)),
                        pltpu.SemaphoreType.DMA((2,))],
        compiler_params=_cp(("arbitrary",)),
        name="moe_combine",
    )(dest_tiles, y_slab, gate, x1, wsg, wsu, wsd, lng, lnb)


def _rope_tables(seq):
    pos = jnp.arange(seq, dtype=F32)
    inv = ROPE_THETA ** (-jnp.arange(0, HEAD_DIM, 2, dtype=F32) / HEAD_DIM)
    ang = pos[:, None] * inv[None, :]
    ang = jnp.concatenate([ang] * (LANES // (HEAD_DIM // 2)), axis=-1)
    first = (np.arange(LANES) % HEAD_DIM) < HEAD_DIM // 2
    sin = jnp.sin(ang)
    return jnp.cos(ang), jnp.where(first, -sin, 0.0), jnp.where(first, 0.0, sin)


def _pack_inproj(w_in, b_in):
    cols = _INPROJ_COLS
    safe = np.where(cols < 0, 0, cols)
    live = jnp.asarray(cols >= 0)
    w = jnp.where(live[None, :], w_in[:, safe], 0.0).astype(BF16)
    b = jnp.where(live, b_in[safe], 0.0)[None, :]
    wt = w_in[:, _INPROJ_TCOLS].T.astype(BF16)
    bt = b_in[_INPROJ_TCOLS][:, None]
    return w, b, wt, bt


def _pack_compress(cmp_pos, cmp_w1, cmp_b1, cmp_w2):
    half = CMP_BLOCK // 2
    g = NSA_KV_HEADS

    def w_half(w1, lo):
        w = w1.reshape(2, CMP_BLOCK, HEAD_DIM, CMP_HIDDEN)[:, lo:lo + half]
        eye = jnp.eye(g, dtype=F32)
        w = jnp.einsum("klde,gh->klgdhe", w, eye)
        return w.reshape(2, half * g * HEAD_DIM, g * CMP_HIDDEN).astype(BF16)

    def p_half(lo):
        p = cmp_pos[:, lo:lo + half]
        p = jnp.broadcast_to(p[:, :, None, :], (2, half, g, HEAD_DIM))
        return p.reshape(2, 1, half * g * HEAD_DIM)

    b1 = jnp.concatenate([cmp_b1] * g, axis=-1)[:, None, :]
    w2 = jnp.concatenate([cmp_w2, cmp_w2], axis=-1).astype(BF16)
    return p_half(0), p_half(half), w_half(cmp_w1, 0), w_half(cmp_w1, half), b1, w2


def _overlap_t(seq):
    ncb = seq // CMP_STRIDE
    n_cmp = ncb - 1
    n_sel = seq // SEL_BLOCK
    c0 = np.arange(ncb)[None, :] * CMP_STRIDE
    s0 = np.arange(LANES)[:, None] * SEL_BLOCK
    ov = np.clip(np.minimum(c0 + CMP_BLOCK, s0 + SEL_BLOCK) - np.maximum(c0, s0), 0, None) / CMP_STRIDE
    ov = ov * (np.arange(ncb)[None, :] < n_cmp) * (np.arange(LANES)[:, None] < n_sel)
    return jnp.asarray(ov, F32)


def _gate_expand():
    r = np.arange(LANES)[None, :, None]
    col = np.arange(NSA_Q)[None, None, :]
    c = np.arange(3)[:, None, None]
    return jnp.asarray(r == (col // HEAD_DIM) * 3 + c, F32)


def _block_plan(counts, n_tok):
    rb = ROW_BLOCK
    nblk = -(-(n_tok * TOP_K + N_EXPERTS * (rb - 1)) // rb)
    padded = (counts + rb - 1) // rb * rb
    pad_end = jnp.cumsum(padded)
    pad_start = (pad_end - padded).astype(jnp.int32)
    first_row = jnp.arange(nblk, dtype=jnp.int32) * rb
    blk_expert = jnp.sum((pad_end[None, :] <= first_row[:, None]).astype(jnp.int32), axis=1)
    blk_expert = jnp.minimum(blk_expert, N_EXPERTS - 1).astype(jnp.int32)
    n_used = (pad_end[-1:] // rb).astype(jnp.int32)
    return pad_start, blk_expert, n_used, nblk


def kernel(x, w_in, b_in, cmp_pos, cmp_w1, cmp_b1, cmp_w2, w_proj_a, w_proj_b, w_out, ln1_g, ln1_b,
           router_w, router_bias, w_gate, w_up, w_down, ws_gate, ws_up, ws_down, ln2_g, ln2_b):
    bsz, seq, d = x.shape
    depth = w_in.shape[0]
    n = bsz * seq
    alpha = float((2 * depth) ** 0.25)
    cos, sa, sb = _rope_tables(seq)
    ovt = _overlap_t(seq)
    eg = _gate_expand()
    ncb = seq // CMP_STRIDE
    x2 = x.reshape(n, d)
    for l in range(depth):
        (qn, qnr, kc, vc, ks, kw, gates, qb, kb, ma, mb, vst, vwt, vbt) = _inproj(
            x2, *_pack_inproj(w_in[l], b_in[l]), cos, sa, sb, seq)
        cw = CMP_STRIDE * NSA_KV
        kcmp, vcmp = _compress(kc.reshape(bsz, ncb, cw), vc.reshape(bsz, ncb, cw),
                               *_pack_compress(cmp_pos[l], cmp_w1[l], cmp_b1[l], cmp_w2[l]))
        ocmp, member = _nsa_cmp(qn, kcmp, vcmp, ovt, bsz, seq)
        osel = _flash("sel", qnr, ks, vst, member, bsz, seq)
        owin = _flash("win", qnr, kw, vwt, member, bsz, seq)
        mmem = _moba_select(qb, _moba_kmean(kb, bsz, seq), bsz, seq)
        ob = _flash("moba", qb, kb, vbt, mmem, bsz, seq)
        x1, x1s = _post(ocmp, osel, owin, ob, gates, ma, mb, x2, eg,
                        w_proj_a[l].astype(BF16), w_proj_b[l].astype(BF16), w_out[l].astype(BF16),
                        ln1_g[l][None, :], ln1_b[l][None, :], alpha)
        top_idx, gate, rank, cnt = _router(x1, router_w[l].T.astype(BF16), router_bias[l][:, None])
        counts = cnt[:, 0].astype(jnp.int32)
        pad_start, blk_expert, n_used, nblk = _block_plan(counts, n)
        dest_tiles = _dest_rows(top_idx, rank, pad_start)
        xs = _dispatch(counts, pad_start, n_used, dest_tiles, x1s, nblk)
        y_slab = _experts(blk_expert, n_used, xs, w_gate, w_up, w_down, l)
        x2 = _combine(dest_tiles, y_slab, gate, x1, ws_gate[l].astype(BF16), ws_up[l].astype(BF16),
                      ws_down[l].astype(BF16), ln2_g[l][None, :], ln2_b[l][None, :], alpha)
    return x2.reshape(bsz, seq, d)
```

```python
import functools

import numpy as np
import jax
import jax.numpy as jnp
from jax import lax
from jax.experimental import pallas as pl
from jax.experimental.pallas import tpu as pltpu

D_MODEL = 1024
HEAD_DIM = 64
NSA_HEADS = 8
NSA_KV_HEADS = 2
MOBA_HEADS = 8
ROPE_THETA = 10000.0
CMP_BLOCK = 32
CMP_STRIDE = 16
CMP_HIDDEN = 128
SEL_BLOCK = 64
SEL_TOPN = 16
WINDOW = 512
MOBA_BLOCK = 256
MOBA_TOPK = 3
N_EXPERTS = 256
TOP_K = 8
N_GROUPS = 8
TOPK_GROUPS = 4
D_EXPERT = 256
ROUTED_SCALE = 2.5
LN_EPS = 1e-5
NEG = -1e30
BIG = 1e30
TAKEN = -3e38

NSA_Q = NSA_HEADS * HEAD_DIM
NSA_KV = NSA_KV_HEADS * HEAD_DIM
MOBA_W = MOBA_HEADS * HEAD_DIM
IN_SPLITS = (NSA_Q, NSA_KV, NSA_KV, NSA_KV, NSA_KV, NSA_KV, NSA_KV, 3 * NSA_HEADS,
             MOBA_W, MOBA_W, MOBA_W, D_MODEL, D_MODEL)

LANES = 128
SUBLANES = 8
SLAB = D_MODEL // LANES
MOBA_SLOTS = LANES // MOBA_HEADS
ROW_BLOCK = 256
DISPATCH_TILE = 128
FLASH_TQ = 512
MOBA_TQ = 1024
ONES_ROWS = 16
LOG2E = 1.4426950408889634
VMEM_LIMIT = 52 * 1024 * 1024

F32 = jnp.float32
BF16 = jnp.bfloat16
NT = (((1,), (1,)), ((), ()))


def _cp(sem, vmem=VMEM_LIMIT):
    return pltpu.CompilerParams(dimension_semantics=sem, vmem_limit_bytes=vmem)


def _iota(shape, axis):
    return lax.broadcasted_iota(jnp.int32, shape, axis)


def _inproj_layout():
    offs = np.concatenate([[0], np.cumsum(IN_SPLITS)])
    groups = []
    pieces = []

    def src(i):
        return np.arange(offs[i], offs[i + 1])

    def dup(i):
        c = src(i)
        return np.concatenate([c[:64], c[:64], c[64:], c[64:]])

    col = 0

    def add(cols, outs):
        nonlocal col
        pieces.append(cols)
        groups.append((col, len(cols), outs))
        col += len(cols)

    add(src(0), [(False, BF16), (True, BF16)])
    add(src(1), [(False, F32)])
    add(src(2), [(False, F32)])
    add(dup(3), [(True, BF16)])
    add(dup(5), [(True, BF16)])
    add(np.concatenate([src(7), -np.ones(LANES - 3 * NSA_HEADS, np.int64)]), [(False, F32)])
    add(src(8), [(True, BF16)])
    add(src(9), [(True, BF16)])
    add(src(11), [(False, F32)])
    add(src(12), [(False, F32)])
    tgroups, tpieces, row = [], [], 0
    for cols in (dup(4), dup(6), src(10)):
        tgroups.append((row, len(cols)))
        tpieces.append(cols)
        row += len(cols)
    return groups, np.concatenate(pieces), tgroups, np.concatenate(tpieces)


_INPROJ_GROUPS, _INPROJ_COLS, _INPROJ_TGROUPS, _INPROJ_TCOLS = _inproj_layout()


def _inproj_kernel(x_ref, w_ref, b_ref, wt_ref, bt_ref, cos_ref, sa_ref, sb_ref, *out_refs):
    xb = x_ref[...].astype(BF16)
    n_tok_major = sum(len(outs) for _, _, outs in _INPROJ_GROUPS)
    for (r0, rows), o_ref in zip(_INPROJ_TGROUPS, out_refs[n_tok_major:]):
        vt = lax.dot_general(wt_ref[r0:r0 + rows, :], xb, NT, preferred_element_type=F32) + bt_ref[r0:r0 + rows, :]
        o_ref[...] = vt.astype(BF16)
    oi = 0
    for c0, width, outs in _INPROJ_GROUPS:
        acc = jnp.dot(xb, w_ref[:, c0:c0 + width], preferred_element_type=F32) + b_ref[:, c0:c0 + width]
        for rope, dtype in outs:
            o_ref = out_refs[oi]
            oi += 1
            if not rope:
                o_ref[...] = acc.astype(dtype)
                continue
            cos = cos_ref[...]
            sa = sa_ref[...]
            sb = sb_ref[...]
            for j in range(0, width, LANES):
                t = acc[:, j:j + LANES]
                r = t * cos + pltpu.roll(t, LANES - HEAD_DIM // 2, 1) * sa + pltpu.roll(t, HEAD_DIM // 2, 1) * sb
                o_ref[:, j:j + LANES] = r.astype(dtype)


def _inproj(x2, w, b, wt, bt, cos, sa, sb, seq, tm=256):
    n = x2.shape[0]
    ctot = w.shape[1]
    rtot = wt.shape[0]
    nt_seq = seq // tm
    out_shape, out_specs = [], []
    for _, width, outs in _INPROJ_GROUPS:
        for _, dtype in outs:
            out_shape.append(jax.ShapeDtypeStruct((n, width), dtype))
            out_specs.append(pl.BlockSpec((tm, width), lambda i: (i, 0)))
    for _, rows in _INPROJ_TGROUPS:
        out_shape.append(jax.ShapeDtypeStruct((rows, n), BF16))
        out_specs.append(pl.BlockSpec((rows, tm), lambda i: (0, i)))
    tab = pl.BlockSpec((tm, LANES), lambda i: (i % nt_seq, 0))
    return pl.pallas_call(
        _inproj_kernel,
        grid=(n // tm,),
        in_specs=[pl.BlockSpec((tm, D_MODEL), lambda i: (i, 0)),
                  pl.BlockSpec((D_MODEL, ctot), lambda i: (0, 0)),
                  pl.BlockSpec((1, ctot), lambda i: (0, 0)),
                  pl.BlockSpec((rtot, D_MODEL), lambda i: (0, 0)),
                  pl.BlockSpec((rtot, 1), lambda i: (0, 0)),
                  tab, tab, tab],
        out_specs=out_specs,
        out_shape=out_shape,
        compiler_params=_cp(("parallel",)),
        name="inproj",
    )(x2, w, b, wt, bt, cos, sa, sb)


def _compress_kernel(kc_ref, vc_ref, ptop_ref, pbot_ref, wtop_ref, wbot_ref, b1_ref, w2_ref, ko_ref, vo_ref):
    ncb = kc_ref.shape[1]
    for kv, (c_ref, o_ref) in enumerate(((kc_ref, ko_ref), (vc_ref, vo_ref))):
        c = c_ref[0]
        a = jnp.dot((c + ptop_ref[kv]).astype(BF16), wtop_ref[kv], preferred_element_type=F32)
        bm = jnp.dot((c + pbot_ref[kv]).astype(BF16), wbot_ref[kv], preferred_element_type=F32)
        hid = jax.nn.gelu(a + pltpu.roll(bm, ncb - 1, 0) + b1_ref[kv])
        for g in range(NSA_KV_HEADS):
            hg = hid[:, g * CMP_HIDDEN:(g + 1) * CMP_HIDDEN].astype(BF16)
            o_ref[0, g] = jnp.dot(hg, w2_ref[kv], preferred_element_type=F32).astype(BF16)


def _compress(kc3, vc3, ptop, pbot, wtop, wbot, b1, w2):
    bsz, ncb, cw = kc3.shape
    hw = NSA_KV_HEADS * CMP_HIDDEN
    full = lambda shp: pl.BlockSpec(shp, lambda b: (0,) * len(shp))
    out = jax.ShapeDtypeStruct((bsz, NSA_KV_HEADS, ncb, LANES), BF16)
    ospec = pl.BlockSpec((1, NSA_KV_HEADS, ncb, LANES), lambda b: (b, 0, 0, 0))
    return pl.pallas_call(
        _compress_kernel,
        grid=(bsz,),
        in_specs=[pl.BlockSpec((1, ncb, cw), lambda b: (b, 0, 0)),
                  pl.BlockSpec((1, ncb, cw), lambda b: (b, 0, 0)),
                  full((2, 1, cw)), full((2, 1, cw)),
                  full((2, cw, hw)), full((2, cw, hw)),
                  full((2, 1, hw)), full((2, CMP_HIDDEN, LANES))],
        out_specs=[ospec, ospec],
        out_shape=[out, out],
        compiler_params=_cp(("parallel",)),
        name="nsa_compress",
    )(kc3, vc3, ptop, pbot, wtop, wbot, b1, w2)


def _rank_rows(val, n_rows):
    rows = _iota(val.shape, 0)
    rank = jnp.zeros(val.shape, F32)
    for i in range(n_rows):
        vi = val[i:i + 1, :]
        ahead = (vi > val) | ((vi == val) & (rows > i))
        rank = rank + jnp.where(ahead, 1.0, 0.0)
    return rank


def _transpose_to_rows(mt):
    tq = mt.shape[1]
    return jnp.concatenate([mt[:, c:c + LANES].T for c in range(0, tq, LANES)], axis=0)


def _nsa_cmp_kernel(q_ref, kc_ref, vc_ref, ov_ref, o_ref, mem_ref, *, tq, n_cmp, n_sel):
    i = pl.program_id(2)
    ncb = kc_ref.shape[2]
    kc = kc_ref[0, 0]
    vc = vc_ref[0, 0]
    pos = i * tq + _iota((tq, ncb), 0)
    blk = _iota((tq, ncb), 1)
    valid = (blk * CMP_STRIDE + CMP_BLOCK - 1 <= pos) & (blk < n_cmp)
    anyv = (i * tq + _iota((tq, 1), 0) >= CMP_BLOCK - 1).astype(F32)
    lane = _iota((1, LANES), 1)
    scale = HEAD_DIM ** -0.5
    psum = jnp.zeros((tq, ncb), F32)
    outs = []
    hg = NSA_HEADS // NSA_KV_HEADS
    for t in range(hg):
        qb = q_ref[:, (t // 2) * LANES:(t // 2 + 1) * LANES]
        half = (lane < HEAD_DIM) if t % 2 == 0 else (lane >= HEAD_DIM)
        qm = jnp.where(half, qb, jnp.zeros_like(qb))
        s = lax.dot_general(qm, kc, NT, preferred_element_type=F32) * scale
        s = jnp.where(valid, s, NEG)
        e = jnp.exp(s - jnp.max(s, axis=1, keepdims=True))
        p = e / jnp.sum(e, axis=1, keepdims=True) * anyv
        psum = psum + p
        outs.append(jnp.dot(p.astype(BF16), vc, preferred_element_type=F32))
    for r in range(hg // 2):
        o_ref[:, r * LANES:(r + 1) * LANES] = jnp.where(lane < HEAD_DIM, outs[2 * r], outs[2 * r + 1]).astype(BF16)

    imp = lax.dot_general(ov_ref[...], psum, NT, preferred_element_type=F32, precision=lax.Precision.HIGHEST)
    imp = imp[:n_sel]
    j = _iota((n_sel, tq), 0)
    cur = (i * tq + _iota((n_sel, tq), 1)) // SEL_BLOCK
    forced = (j == 0) | (j == cur) | (j == cur - 1)
    val = jnp.where(forced, BIG, jnp.where(j <= cur, imp, NEG))
    rank = _rank_rows(val, n_sel)
    member = jnp.where((rank < float(min(SEL_TOPN, n_sel))) & (j <= cur), 1.0, 0.0)
    if n_sel < LANES:
        member = jnp.concatenate([member, jnp.zeros((LANES - n_sel, tq), F32)], axis=0)
    mem_ref[0, 0] = member


def _nsa_cmp(qn, kcmp, vcmp, ovt, bsz, seq, tq=256):
    n = qn.shape[0]
    nq = seq // tq
    ncb = kcmp.shape[2]
    n_sel = seq // SEL_BLOCK
    gw = NSA_Q // NSA_KV_HEADS
    kern = functools.partial(_nsa_cmp_kernel, tq=tq, n_cmp=ncb - 1, n_sel=n_sel)
    kvspec = pl.BlockSpec((1, 1, ncb, LANES), lambda b, g, i: (b, g, 0, 0))
    return pl.pallas_call(
        kern,
        grid=(bsz, NSA_KV_HEADS, nq),
        in_specs=[pl.BlockSpec((tq, gw), lambda b, g, i: (b * nq + i, g)),
                  kvspec, kvspec,
                  pl.BlockSpec((LANES, ncb), lambda b, g, i: (0, 0))],
        out_specs=[pl.BlockSpec((tq, gw), lambda b, g, i: (b * nq + i, g)),
                   pl.BlockSpec((1, 1, LANES, tq), lambda b, g, i: (b, g, 0, i))],
        out_shape=[jax.ShapeDtypeStruct((n, NSA_Q), BF16),
                   jax.ShapeDtypeStruct((bsz, NSA_KV_HEADS, LANES, seq), F32)],
        compiler_params=_cp(("parallel", "parallel", "parallel")),
        name="nsa_cmp_select",
    )(qn, kcmp, vcmp, ovt)


def _moba_kmean_kernel(k_ref, o_ref):
    nblk = o_ref.shape[1]
    k = k_ref[...].astype(F32).reshape(nblk, MOBA_BLOCK, MOBA_W)
    o_ref[0] = jnp.sum(k, axis=1) * (1.0 / MOBA_BLOCK)


def _moba_kmean(kb, bsz, seq):
    nblk = seq // MOBA_BLOCK
    return pl.pallas_call(
        _moba_kmean_kernel,
        grid=(bsz,),
        in_specs=[pl.BlockSpec((seq, MOBA_W), lambda b: (b, 0))],
        out_specs=pl.BlockSpec((1, nblk, MOBA_W), lambda b: (b, 0, 0)),
        out_shape=jax.ShapeDtypeStruct((bsz, nblk, MOBA_W), F32),
        compiler_params=_cp(("parallel",)),
        name="moba_kmean",
    )(kb)


def _moba_select_kernel(q_ref, km_ref, mem_ref, *, tq, nblk):
    i = pl.program_id(1)
    km = km_ref[0]
    if nblk < MOBA_SLOTS:
        km = jnp.concatenate([km, jnp.zeros((MOBA_SLOTS - nblk, MOBA_W), F32)], axis=0)
    kmt = jnp.concatenate([km] * MOBA_HEADS, axis=0)
    r = _iota((LANES, MOBA_W), 0)
    c = _iota((LANES, MOBA_W), 1)
    kmt = jnp.where(r // MOBA_SLOTS == c // HEAD_DIM, kmt, 0.0).astype(BF16)
    gate = lax.dot_general(kmt, q_ref[...], NT, preferred_element_type=F32)
    members = []
    for h in range(MOBA_HEADS):
        sub = gate[h * MOBA_SLOTS:(h + 1) * MOBA_SLOTS]
        j = _iota((MOBA_SLOTS, tq), 0)
        cur = (i * tq + _iota((MOBA_SLOTS, tq), 1)) // MOBA_BLOCK
        past = j < cur
        val = jnp.where(past, sub, NEG)
        rank = _rank_rows(val, min(nblk, MOBA_SLOTS))
        members.append(jnp.where(((rank < float(MOBA_TOPK)) & past) | (j == cur), 1.0, 0.0))
    mem_ref[...] = jnp.concatenate(members, axis=0)


def _moba_select(qb, kmean, bsz, seq, tq=256):
    n = qb.shape[0]
    nq = seq // tq
    nblk = seq // MOBA_BLOCK
    kern = functools.partial(_moba_select_kernel, tq=tq, nblk=nblk)
    return pl.pallas_call(
        kern,
        grid=(bsz, nq),
        in_specs=[pl.BlockSpec((tq, MOBA_W), lambda b, i: (b * nq + i, 0)),
                  pl.BlockSpec((1, nblk, MOBA_W), lambda b, i: (b, 0, 0))],
        out_specs=pl.BlockSpec((LANES, tq), lambda b, i: (0, b * nq + i)),
        out_shape=jax.ShapeDtypeStruct((LANES, n), F32),
        compiler_params=_cp(("parallel", "parallel")),
        name="moba_select",
    )(qb, kmean)


def _flash_kernel(qi_ref, kj_ref, first_ref, last_ref, edge_ref, q_ref, k_ref, vt_ref, mem_ref, o_ref,
                  m_sc, acc_sc, *, mode, nhq, tq, tk):
    hg = pl.program_id(1)
    p = pl.program_id(2)
    i = qi_ref[p]
    jj = kj_ref[p]

    @pl.when(first_ref[p] == 1)
    def _():
        m_sc[...] = jnp.full(m_sc.shape, NEG, F32)
        acc_sc[...] = jnp.zeros(acc_sc.shape, F32)

    def member_rows(first_row, keys_per_row):
        rows = [mem_ref[pl.ds(first_row + r, 1), :] > 0.5 for r in range(tk // keys_per_row)]
        return jnp.concatenate([jnp.broadcast_to(r, (keys_per_row, tq)) for r in rows], axis=0)

    def step(positional):
        allowed = None
        if positional:
            kpos = jj * tk + _iota((tk, tq), 0)
            qpos = i * tq + _iota((tk, tq), 1)
            allowed = kpos <= qpos
            if mode == "win":
                allowed = allowed & (kpos > qpos - WINDOW)
        if mode == "sel":
            memb = member_rows(jj * (tk // SEL_BLOCK), SEL_BLOCK)
            allowed = memb if allowed is None else allowed & memb
        k = k_ref[...]
        vt1 = jnp.concatenate([vt_ref[...], jnp.ones((ONES_ROWS, tk), BF16)], axis=0)
        lane = _iota((1, LANES), 1)
        qms = []
        for t in range(nhq):
            qb = q_ref[:, (t // 2) * LANES:(t // 2 + 1) * LANES].astype(F32)
            half = (lane < HEAD_DIM) if t % 2 == 0 else (lane >= HEAD_DIM)
            qms.append(jnp.where(half, qb * (HEAD_DIM ** -0.5 * LOG2E), 0.0).astype(BF16))
        s_all = lax.dot_general(k, jnp.concatenate(qms, axis=0), NT, preferred_element_type=F32)
        prs, alphas = [], []
        for t in range(nhq):
            s = s_all[:, t * tq:(t + 1) * tq]
            ok = allowed
            if mode == "moba":
                memb = member_rows((hg * 2 + t) * MOBA_SLOTS + jj * (tk // MOBA_BLOCK), MOBA_BLOCK)
                ok = memb if ok is None else ok & memb
            if ok is not None:
                s = jnp.where(ok, s, NEG)
            m_old = m_sc[:, t * tq:(t + 1) * tq]
            m_new = jnp.maximum(m_old, jnp.max(s, axis=0, keepdims=True))
            prs.append(jnp.exp2(s - m_new).astype(BF16))
            alphas.append(jnp.exp2(m_old - m_new))
            m_sc[:, t * tq:(t + 1) * tq] = m_new
        pv = jnp.dot(vt1, jnp.concatenate(prs, axis=1), preferred_element_type=F32)
        acc_sc[...] = jnp.concatenate(alphas, axis=1) * acc_sc[...] + pv

    @pl.when(edge_ref[p] == 1)
    def _():
        step(True)

    @pl.when(edge_ref[p] == 0)
    def _():
        step(False)

    @pl.when(last_ref[p] == 1)
    def _():
        for r in range(nhq // 2):
            a0 = acc_sc[:, 2 * r * tq:(2 * r + 1) * tq]
            a1 = acc_sc[:, (2 * r + 1) * tq:(2 * r + 2) * tq]
            o0 = a0[:HEAD_DIM] / a0[LANES:LANES + 1]
            o1 = a1[HEAD_DIM:LANES] / a1[LANES:LANES + 1]
            ot = jnp.concatenate([o0, o1], axis=0)
            o_ref[:, r * LANES:(r + 1) * LANES] = _transpose_to_rows(ot).astype(BF16)


def _pair_tables(nq, tq, tk, window):
    qi, kj, first, last, edge = [], [], [], [], []
    for i in range(nq):
        q_lo, q_hi = i * tq, i * tq + tq - 1
        lo = 0 if window is None else max(0, (q_lo - window + 1) // tk)
        hi = q_hi // tk
        for j in range(lo, hi + 1):
            k_lo, k_hi = j * tk, j * tk + tk - 1
            inside = k_hi <= q_lo and (window is None or k_lo > q_hi - window)
            qi.append(i)
            kj.append(j)
            first.append(int(j == lo))
            last.append(int(j == hi))
            edge.append(int(not inside))
    mk = lambda a: jnp.asarray(np.asarray(a, np.int32))
    return mk(qi), mk(kj), mk(first), mk(last), mk(edge)


def _flash(mode, q, k, vt, mem, bsz, seq, tq=FLASH_TQ, tk=MOBA_BLOCK):
    assert tk % MOBA_BLOCK == 0 and tk % SEL_BLOCK == 0 and seq % tq == 0 and seq % tk == 0
    n = q.shape[0]
    nq = seq // tq
    nk = seq // tk
    if mode == "moba":
        ngrp, nhq = MOBA_HEADS // 2, 2
        mem_spec = pl.BlockSpec((LANES, tq), lambda b, g, p, qi, kj, fi, la, ed: (0, b * nq + qi[p]))
    else:
        ngrp, nhq = NSA_KV_HEADS, NSA_HEADS // NSA_KV_HEADS
        mem_spec = pl.BlockSpec((None, None, LANES, tq), lambda b, g, p, qi, kj, fi, la, ed: (b, g, 0, qi[p]))
    tables = _pair_tables(nq, tq, tk, WINDOW if mode == "win" else None)
    npairs = int(tables[0].shape[0])
    qw = nhq // 2 * LANES
    kern = functools.partial(_flash_kernel, mode=mode, nhq=nhq, tq=tq, tk=tk)
    kspec = pl.BlockSpec((tk, LANES), lambda b, g, p, qi, kj, fi, la, ed: (b * nk + kj[p], g))
    vtspec = pl.BlockSpec((LANES, tk), lambda b, g, p, qi, kj, fi, la, ed: (g, b * nk + kj[p]))
    qspec = pl.BlockSpec((tq, qw), lambda b, g, p, qi, kj, fi, la, ed: (b * nq + qi[p], g))
    grid_spec = pltpu.PrefetchScalarGridSpec(
        num_scalar_prefetch=5,
        grid=(bsz, ngrp, npairs),
        in_specs=[qspec, kspec, vtspec, mem_spec],
        out_specs=qspec,
        scratch_shapes=[pltpu.VMEM((1, nhq * tq), F32),
                        pltpu.VMEM((LANES + ONES_ROWS, nhq * tq), F32)],
    )
    return pl.pallas_call(
        kern,
        grid_spec=grid_spec,
        out_shape=jax.ShapeDtypeStruct((n, ngrp * qw), BF16),
        compiler_params=_cp(("parallel", "parallel", "arbitrary")),
        name="flash_" + mode,
    )(*tables, q, k, vt, mem)


def _layer_norm(z, g, b):
    mu = jnp.mean(z, axis=-1, keepdims=True)
    zc = z - mu
    var = jnp.mean(zc * zc, axis=-1, keepdims=True)
    return zc * lax.rsqrt(var + LN_EPS) * g + b


def _store_slabs(slab_ref, val, rows):
    for c in range(SLAB):
        slab_ref[pl.ds(c, rows, stride=SLAB), :] = val[:, c * LANES:(c + 1) * LANES]


def _post_kernel(ocmp_ref, osel_ref, owin_ref, ob_ref, g_ref, ma_ref, mb_ref, x_ref,
                 eg_ref, wpa_ref, wpb_ref, wout_ref, lng_ref, lnb_ref, x1_ref, x1s_ref, *, alpha, tm):
    sig = jax.nn.sigmoid(g_ref[...])
    oa = jnp.zeros((tm, NSA_Q), F32)
    for c, o_ref in enumerate((ocmp_ref, osel_ref, owin_ref)):
        gexp = jnp.dot(sig, eg_ref[c], preferred_element_type=F32, precision=lax.Precision.HIGHEST)
        oa = oa + gexp * o_ref[...].astype(F32)
    pa = jnp.dot(oa.astype(BF16), wpa_ref[...], preferred_element_type=F32)
    pb = jnp.dot(ob_ref[...], wpb_ref[...], preferred_element_type=F32)
    merged = jax.nn.sigmoid(ma_ref[...]) * pa + jax.nn.sigmoid(mb_ref[...]) * pb
    y = jnp.dot(merged.astype(BF16), wout_ref[...], preferred_element_type=F32)
    x1 = _layer_norm(alpha * x_ref[...] + y, lng_ref[...], lnb_ref[...])
    x1_ref[...] = x1
    _store_slabs(x1s_ref, x1, tm)


def _post(ocmp, osel, owin, ob, gates, ma, mb, x2, eg, wpa, wpb, wout, lng, lnb, alpha, tm=256):
    n = x2.shape[0]
    row = lambda w: pl.BlockSpec((tm, w), lambda i: (i, 0))
    full = lambda shp: pl.BlockSpec(shp, lambda i: (0,) * len(shp))
    kern = functools.partial(_post_kernel, alpha=alpha, tm=tm)
    return pl.pallas_call(
        kern,
        grid=(n // tm,),
        in_specs=[row(NSA_Q), row(NSA_Q), row(NSA_Q), row(MOBA_W), row(LANES), row(D_MODEL), row(D_MODEL),
                  row(D_MODEL), full((3, LANES, NSA_Q)), full((NSA_Q, D_MODEL)), full((MOBA_W, D_MODEL)),
                  full((D_MODEL, D_MODEL)), full((1, D_MODEL)), full((1, D_MODEL))],
        out_specs=[row(D_MODEL), pl.BlockSpec((tm * SLAB, LANES), lambda i: (i, 0))],
        out_shape=[jax.ShapeDtypeStruct((n, D_MODEL), F32), jax.ShapeDtypeStruct((n * SLAB, LANES), F32)],
        compiler_params=_cp(("parallel",)),
        name="post_attention",
    )(ocmp, osel, owin, ob, gates, ma, mb, x2, eg, wpa, wpb, wout, lng, lnb)


def _first_argmax_rows(cur, rows, n_rows):
    mx = jnp.max(cur, axis=0, keepdims=True)
    idx = jnp.min(jnp.where(cur == mx, rows, n_rows), axis=0, keepdims=True)
    return mx, idx


def _router_kernel(x_ref, rwt_ref, bias_ref, idx_ref, gate_ref, rank_ref, cnt_ref, carry_sc, *, tm):
    i = pl.program_id(0)

    @pl.when(i == 0)
    def _():
        carry_sc[...] = jnp.zeros(carry_sc.shape, F32)

    logits = lax.dot_general(rwt_ref[...], x_ref[...].astype(BF16), NT, preferred_element_type=F32)
    sc = jax.nn.sigmoid(logits)
    biased = sc + bias_ref[...]
    gsz = N_EXPERTS // N_GROUPS
    grow = _iota((gsz, tm), 0)
    gscore = []
    for g in range(N_GROUPS):
        grp = biased[g * gsz:(g + 1) * gsz]
        m1, i1 = _first_argmax_rows(grp, grow, gsz)
        m2 = jnp.max(jnp.where(grow == i1, TAKEN, grp), axis=0, keepdims=True)
        gscore.append(m1 + m2)
    gs = jnp.concatenate(gscore, axis=0)
    keep = _rank_rows(gs, N_GROUPS) < float(TOPK_GROUPS)
    cur = jnp.concatenate(
        [jnp.where(keep[g:g + 1], biased[g * gsz:(g + 1) * gsz], NEG) for g in range(N_GROUPS)], axis=0)
    rows = _iota((N_EXPERTS, tm), 0)
    sels, tops, idxs = [], [], []
    for _ in range(TOP_K):
        _, ik = _first_argmax_rows(cur, rows, N_EXPERTS)
        selk = rows == ik
        tops.append(jnp.sum(jnp.where(selk, sc, 0.0), axis=0, keepdims=True))
        cur = jnp.where(selk, TAKEN, cur)
        sels.append(selk)
        idxs.append(ik)
    denom = tops[0]
    for tk_ in tops[1:]:
        denom = denom + tk_
    selall = jnp.where(cur == TAKEN, 1.0, 0.0)
    upper = jnp.where(_iota((tm, tm), 0) < _iota((tm, tm), 1), 1.0, 0.0).astype(BF16)
    before = jnp.dot(selall.astype(BF16), upper, preferred_element_type=F32) + carry_sc[...]
    idx_ref[...] = jnp.concatenate(idxs, axis=0)
    gate_ref[...] = jnp.concatenate([tk_ / denom * ROUTED_SCALE for tk_ in tops], axis=0)
    rank_ref[...] = jnp.concatenate(
        [jnp.sum(jnp.where(s, before, 0.0), axis=0, keepdims=True) for s in sels], axis=0).astype(jnp.int32)
    carry_sc[...] = carry_sc[...] + jnp.sum(selall, axis=1, keepdims=True)
    cnt_ref[...] = jnp.broadcast_to(carry_sc[...], cnt_ref.shape)


def _router(x1, rwt, bias, tm=256):
    n = x1.shape[0]
    kspec = pl.BlockSpec((TOP_K, tm), lambda i: (0, i))
    return pl.pallas_call(
        functools.partial(_router_kernel, tm=tm),
        grid=(n // tm,),
        in_specs=[pl.BlockSpec((tm, D_MODEL), lambda i: (i, 0)),
                  pl.BlockSpec((N_EXPERTS, D_MODEL), lambda i: (0, 0)),
                  pl.BlockSpec((N_EXPERTS, 1), lambda i: (0, 0))],
        out_specs=[kspec, kspec, kspec, pl.BlockSpec((N_EXPERTS, LANES), lambda i: (0, 0))],
        out_shape=[jax.ShapeDtypeStruct((TOP_K, n), jnp.int32), jax.ShapeDtypeStruct((TOP_K, n), F32),
                   jax.ShapeDtypeStruct((TOP_K, n), jnp.int32), jax.ShapeDtypeStruct((N_EXPERTS, LANES), F32)],
        scratch_shapes=[pltpu.VMEM((N_EXPERTS, 1), F32)],
        compiler_params=_cp(("arbitrary",)),
        name="router",
    )(x1, rwt, bias)


def _dest_kernel(idx_ref, rank_ref, pstart_ref, o_ref, *, tm):
    rows = _iota((N_EXPERTS, tm), 0)
    pstart = pstart_ref[...]
    for k in range(TOP_K):
        hit = rows == idx_ref[k:k + 1, :]
        base = jnp.sum(jnp.where(hit, pstart, 0.0), axis=0, keepdims=True)
        o_ref[:, k * tm:(k + 1) * tm] = base.astype(jnp.int32) + rank_ref[k:k + 1, :]


def _dest_rows(top_idx, rank, pad_start, tm=DISPATCH_TILE):
    n = top_idx.shape[1]
    kspec = pl.BlockSpec((TOP_K, tm), lambda i: (0, i))
    out = pl.pallas_call(
        functools.partial(_dest_kernel, tm=tm),
        grid=(n // tm,),
        in_specs=[kspec, kspec, pl.BlockSpec((N_EXPERTS, 1), lambda i: (0, 0))],
        out_specs=pl.BlockSpec((None, 1, TOP_K * tm), lambda i: (i, 0, 0)),
        out_shape=jax.ShapeDtypeStruct((n // tm, 1, TOP_K * tm), jnp.int32),
        compiler_params=_cp(("parallel",)),
        name="moe_dest_rows",
    )(top_idx, rank, pad_start.astype(F32)[:, None])
    return out.reshape(n // tm, TOP_K, tm).transpose(0, 2, 1).reshape(n // tm, tm * TOP_K)


def _issue_per_token(tm, issue):
    def body(t, carry):
        for k in range(TOP_K):
            issue(t, k)
        return carry
    lax.fori_loop(0, tm, body, 0)


def _dispatch_kernel(cnt_ref, pstart_ref, nused_ref, dest_hbm, x_ref, xs_hbm, idx_sm, zero_sc, idx_sem, row_sem,
                     zero_sem, *, ntile, tm, nblk):
    i = pl.program_id(0)
    slot = i % 2
    nrow = TOP_K * tm

    def idx_copy(t, s):
        return pltpu.make_async_copy(dest_hbm.at[t], idx_sm.at[s], idx_sem.at[s])

    def zero_copy(row, rows):
        return pltpu.make_async_copy(zero_sc.at[pl.ds(0, rows * SLAB)],
                                     xs_hbm.at[pl.ds(pl.multiple_of(row * SLAB, SLAB), rows * SLAB)], zero_sem.at[0])

    def zero_fill(act):
        def per_expert(e, carry):
            cnt = cnt_ref[e]
            npad = (ROW_BLOCK - cnt % ROW_BLOCK) % ROW_BLOCK
            row = pstart_ref[e] + cnt
            piece = ROW_BLOCK // 2
            while piece >= 1:
                @pl.when((npad & piece) != 0)
                def _(row=row, piece=piece):
                    act(zero_copy(row, piece))
                row = row + (npad & piece)
                piece //= 2
            return carry
        lax.fori_loop(0, N_EXPERTS, per_expert, 0)

        def per_block(bk, carry):
            act(zero_copy(bk * ROW_BLOCK, ROW_BLOCK))
            return carry
        lax.fori_loop(nused_ref[0], nblk, per_block, 0)

    @pl.when(i == 0)
    def _():
        idx_copy(0, 0).start()
        if ntile > 1:
            idx_copy(1, 1).start()
        zero_sc[...] = jnp.zeros(zero_sc.shape, F32)
        zero_fill(lambda cp: cp.start())
        zero_fill(lambda cp: cp.wait())

    idx_copy(i, slot).wait()

    def issue(t, k):
        src = x_ref.at[pl.ds(pl.multiple_of(t * SLAB, SLAB), SLAB)]
        dst = xs_hbm.at[pl.ds(pl.multiple_of(idx_sm[slot, t * TOP_K + k] * SLAB, SLAB), SLAB)]
        pltpu.make_async_copy(src, dst, row_sem.at[0]).start()
    _issue_per_token(tm, issue)

    @pl.when(i + 2 < ntile)
    def _():
        idx_copy(i + 2, slot).start()

    for _ in range(TOP_K):
        pltpu.make_async_copy(x_ref, xs_hbm.at[pl.ds(0, tm * SLAB)], row_sem.at[0]).wait()


def _dispatch(counts, pad_start, n_used, dest_tiles, x1s, nblk, tm=DISPATCH_TILE):
    ntile = dest_tiles.shape[0]
    grid_spec = pltpu.PrefetchScalarGridSpec(
        num_scalar_prefetch=3,
        grid=(ntile,),
        in_specs=[pl.BlockSpec(memory_space=pl.ANY),
                  pl.BlockSpec((tm * SLAB, LANES), lambda i, cn, ps, nu: (i, 0))],
        out_specs=pl.BlockSpec(memory_space=pl.ANY),
        scratch_shapes=[pltpu.SMEM((2, TOP_K * tm), jnp.int32),
                        pltpu.VMEM((ROW_BLOCK * SLAB, LANES), F32),
                        pltpu.SemaphoreType.DMA((2,)),
                        pltpu.SemaphoreType.DMA((1,)),
                        pltpu.SemaphoreType.DMA((1,))],
    )
    return pl.pallas_call(
        functools.partial(_dispatch_kernel, ntile=ntile, tm=tm, nblk=nblk),
        grid_spec=grid_spec,
        out_shape=jax.ShapeDtypeStruct((nblk * ROW_BLOCK * SLAB, LANES), F32),
        compiler_params=_cp(("arbitrary",)),
        name="moe_dispatch",
    )(counts, pad_start, n_used, dest_tiles, x1s)


def _expert_kernel(be_ref, nused_ref, xs_ref, wg_ref, wu_ref, wd_ref, y_ref, xmat, wgu_sc, wd_sc):
    i = pl.program_id(0)
    rb = ROW_BLOCK

    @pl.when(i < nused_ref[0])
    def _():
        e = be_ref[i]
        prev = be_ref[jnp.maximum(i - 1, 0)]

        @pl.when((i == 0) | (e != prev))
        def _():
            wgu_sc[:, :D_EXPERT] = wg_ref[0, 0].astype(BF16)
            wgu_sc[:, D_EXPERT:] = wu_ref[0, 0].astype(BF16)
            wd_sc[...] = wd_ref[0, 0].astype(BF16)

        for c in range(SLAB):
            xmat[:, c * LANES:(c + 1) * LANES] = xs_ref[pl.ds(c, rb, stride=SLAB), :].astype(BF16)
        hgu = jnp.dot(xmat[...], wgu_sc[...], preferred_element_type=F32)
        h = jax.nn.silu(hgu[:, :D_EXPERT]) * hgu[:, D_EXPERT:]
        y = jnp.dot(h.astype(BF16), wd_sc[...], preferred_element_type=F32)
        _store_slabs(y_ref, y, rb)

    @pl.when(i >= nused_ref[0])
    def _():
        y_ref[...] = jnp.zeros(y_ref.shape, F32)


def _experts(blk_expert, n_used, xs, w_gate, w_up, w_down, layer):
    nblk = blk_expert.shape[0]
    rb = ROW_BLOCK
    blk = lambda i, nu: jnp.minimum(i, nu[0] - 1)
    wspec = lambda shp: pl.BlockSpec((1, 1) + shp, lambda i, be, nu: (layer, be[blk(i, nu)], 0, 0))
    grid_spec = pltpu.PrefetchScalarGridSpec(
        num_scalar_prefetch=2,
        grid=(nblk,),
        in_specs=[pl.BlockSpec((rb * SLAB, LANES), lambda i, be, nu: (blk(i, nu), 0)),
                  wspec((D_MODEL, D_EXPERT)), wspec((D_MODEL, D_EXPERT)), wspec((D_EXPERT, D_MODEL))],
        out_specs=pl.BlockSpec((rb * SLAB, LANES), lambda i, be, nu: (i, 0)),
        scratch_shapes=[pltpu.VMEM((rb, D_MODEL), BF16),
                        pltpu.VMEM((D_MODEL, 2 * D_EXPERT), BF16),
                        pltpu.VMEM((D_EXPERT, D_MODEL), BF16)],
    )
    return pl.pallas_call(
        _expert_kernel,
        grid_spec=grid_spec,
        out_shape=jax.ShapeDtypeStruct((nblk * rb * SLAB, LANES), F32),
        compiler_params=_cp(("arbitrary",)),
        name="moe_experts",
    )(blk_expert, n_used, xs, w_gate, w_up, w_down)


def _combine_kernel(dest_hbm, y_hbm, gate_ref, x1_ref, wsg_ref, wsu_ref, wsd_ref, lng_ref, lnb_ref, o_ref,
                    idx_sm, ybuf, gbuf, acc_sc, idx_sem, row_sem, *, ntile, tm, alpha):
    i = pl.program_id(0)
    slot = i % 2
    nrow = TOP_K * tm

    def idx_copy(t, s):
        return pltpu.make_async_copy(dest_hbm.at[t], idx_sm.at[s], idx_sem.at[s])

    def gather(s):
        def issue(t, k):
            src = y_hbm.at[pl.ds(pl.multiple_of(idx_sm[s, t * TOP_K + k] * SLAB, SLAB), SLAB)]
            dst = ybuf.at[s, pl.ds(pl.multiple_of((k * tm + t) * SLAB, SLAB), SLAB)]
            pltpu.make_async_copy(src, dst, row_sem.at[s]).start()
        _issue_per_token(tm, issue)

    @pl.when(i == 0)
    def _():
        idx_copy(0, 0).start()
        idx_copy(0, 0).wait()
        gather(0)
        if ntile > 1:
            idx_copy(1, 1).start()

    @pl.when(i + 1 < ntile)
    def _():
        idx_copy(i + 1, 1 - slot).wait()
        gather(1 - slot)

    @pl.when(i + 2 < ntile)
    def _():
        idx_copy(i + 2, slot).start()

    x1 = x1_ref[...]
    xb = x1.astype(BF16)
    hs = jax.nn.silu(jnp.dot(xb, wsg_ref[...], preferred_element_type=F32)) * jnp.dot(
        xb, wsu_ref[...], preferred_element_type=F32)
    shared = jnp.dot(hs.astype(BF16), wsd_ref[...], preferred_element_type=F32)

    for k in range(TOP_K):
        gcol = _transpose_to_rows(jnp.broadcast_to(gate_ref[k:k + 1, :], (LANES, tm)))
        for c in range(SLAB):
            gbuf[pl.ds(k * tm * SLAB + c, tm, stride=SLAB), :] = gcol

    pltpu.make_async_copy(y_hbm.at[pl.ds(0, nrow * SLAB)], ybuf.at[slot], row_sem.at[slot]).wait()
    acc = ybuf[slot, pl.ds(0, tm * SLAB), :] * gbuf[pl.ds(0, tm * SLAB), :]
    for k in range(1, TOP_K):
        acc = acc + ybuf[slot, pl.ds(k * tm * SLAB, tm * SLAB), :] * gbuf[pl.ds(k * tm * SLAB, tm * SLAB), :]
    acc_sc[...] = acc
    routed = jnp.concatenate([acc_sc[pl.ds(c, tm, stride=SLAB), :] for c in range(SLAB)], axis=1)
    o_ref[...] = _layer_norm(alpha * x1 + routed + shared, lng_ref[...], lnb_ref[...])


def _combine(dest_tiles, y_slab, gate, x1, wsg, wsu, wsd, lng, lnb, alpha, tm=DISPATCH_TILE):
    n = x1.shape[0]
    ntile = n // tm
    nrow = TOP_K * tm
    full = lambda shp: pl.BlockSpec(shp, lambda i: (0,) * len(shp))
    kern = functools.partial(_combine_kernel, ntile=ntile, tm=tm, alpha=alpha)
    return pl.pallas_call(
        kern,
        grid=(ntile,),
        in_specs=[pl.BlockSpec(memory_space=pl.ANY),
                  pl.BlockSpec(memory_space=pl.ANY),
                  pl.BlockSpec((TOP_K, tm), lambda i: (0, i)),
                  pl.BlockSpec((tm, D_MODEL), lambda i: (i, 0)),
                  full((D_MODEL, D_EXPERT)), full((D_MODEL, D_EXPERT)), full((D_EXPERT, D_MODEL)),
                  full((1, D_MODEL)), full((1, D_MODEL))],
        out_specs=pl.BlockSpec((tm, D_MODEL), lambda i: (i, 0)),
        out_shape=jax.ShapeDtypeStruct((n, D_MODEL), F32),
        scratch_shapes=[pltpu.SMEM((2, nrow), jnp.int32),
                        pltpu.VMEM((2, nrow * SLAB, LANES), F32),
                        pltpu.VMEM((nrow * SLAB, LANES), F32),
                        pltpu.VMEM((tm * SLAB, LANES), F32),
                        pltpu.SemaphoreType.DMA((2,)),
                        pltpu.SemaphoreType.DMA((2,))],
        compiler_params=_cp(("arbitrary",)),
        name="moe_combine",
    )(dest_tiles, y_slab, gate, x1, wsg, wsu, wsd, lng, lnb)


def _rope_tables(seq):
    pos = jnp.arange(seq, dtype=F32)
    inv = ROPE_THETA ** (-jnp.arange(0, HEAD_DIM, 2, dtype=F32) / HEAD_DIM)
    ang = pos[:, None] * inv[None, :]
    ang = jnp.concatenate([ang] * (LANES // (HEAD_DIM // 2)), axis=-1)
    first = (np.arange(LANES) % HEAD_DIM) < HEAD_DIM // 2
    sin = jnp.sin(ang)
    return jnp.cos(ang), jnp.where(first, -sin, 0.0), jnp.where(first, 0.0, sin)


def _pack_inproj(w_in, b_in):
    cols = _INPROJ_COLS
    safe = np.where(cols < 0, 0, cols)
    live = jnp.asarray(cols >= 0)
    w = jnp.where(live[None, :], w_in[:, safe], 0.0).astype(BF16)
    b = jnp.where(live, b_in[safe], 0.0)[None, :]
    wt = w_in[:, _INPROJ_TCOLS].T.astype(BF16)
    bt = b_in[_INPROJ_TCOLS][:, None]
    return w, b, wt, bt


def _pack_compress(cmp_pos, cmp_w1, cmp_b1, cmp_w2):
    half = CMP_BLOCK // 2
    g = NSA_KV_HEADS

    def w_half(w1, lo):
        w = w1.reshape(2, CMP_BLOCK, HEAD_DIM, CMP_HIDDEN)[:, lo:lo + half]
        eye = jnp.eye(g, dtype=F32)
        w = jnp.einsum("klde,gh->klgdhe", w, eye)
        return w.reshape(2, half * g * HEAD_DIM, g * CMP_HIDDEN).astype(BF16)

    def p_half(lo):
        p = cmp_pos[:, lo:lo + half]
        p = jnp.broadcast_to(p[:, :, None, :], (2, half, g, HEAD_DIM))
        return p.reshape(2, 1, half * g * HEAD_DIM)

    b1 = jnp.concatenate([cmp_b1] * g, axis=-1)[:, None, :]
    w2 = jnp.concatenate([cmp_w2, cmp_w2], axis=-1).astype(BF16)
    return p_half(0), p_half(half), w_half(cmp_w1, 0), w_half(cmp_w1, half), b1, w2


def _overlap_t(seq):
    ncb = seq // CMP_STRIDE
    n_cmp = ncb - 1
    n_sel = seq // SEL_BLOCK
    c0 = np.arange(ncb)[None, :] * CMP_STRIDE
    s0 = np.arange(LANES)[:, None] * SEL_BLOCK
    ov = np.clip(np.minimum(c0 + CMP_BLOCK, s0 + SEL_BLOCK) - np.maximum(c0, s0), 0, None) / CMP_STRIDE
    ov = ov * (np.arange(ncb)[None, :] < n_cmp) * (np.arange(LANES)[:, None] < n_sel)
    return jnp.asarray(ov, F32)


def _gate_expand():
    r = np.arange(LANES)[None, :, None]
    col = np.arange(NSA_Q)[None, None, :]
    c = np.arange(3)[:, None, None]
    return jnp.asarray(r == (col // HEAD_DIM) * 3 + c, F32)


def _block_plan(counts, n_tok):
    rb = ROW_BLOCK
    nblk = -(-(n_tok * TOP_K + N_EXPERTS * (rb - 1)) // rb)
    padded = (counts + rb - 1) // rb * rb
    pad_end = jnp.cumsum(padded)
    pad_start = (pad_end - padded).astype(jnp.int32)
    first_row = jnp.arange(nblk, dtype=jnp.int32) * rb
    blk_expert = jnp.sum((pad_end[None, :] <= first_row[:, None]).astype(jnp.int32), axis=1)
    blk_expert = jnp.minimum(blk_expert, N_EXPERTS - 1).astype(jnp.int32)
    n_used = (pad_end[-1:] // rb).astype(jnp.int32)
    return pad_start, blk_expert, n_used, nblk


def kernel(x, w_in, b_in, cmp_pos, cmp_w1, cmp_b1, cmp_w2, w_proj_a, w_proj_b, w_out, ln1_g, ln1_b,
           router_w, router_bias, w_gate, w_up, w_down, ws_gate, ws_up, ws_down, ln2_g, ln2_b):
    bsz, seq, d = x.shape
    depth = w_in.shape[0]
    n = bsz * seq
    alpha = float((2 * depth) ** 0.25)
    cos, sa, sb = _rope_tables(seq)
    ovt = _overlap_t(seq)
    eg = _gate_expand()
    ncb = seq // CMP_STRIDE
    x2 = x.reshape(n, d)
    for l in range(depth):
        (qn, qnr, kc, vc, ks, kw, gates, qb, kb, ma, mb, vst, vwt, vbt) = _inproj(
            x2, *_pack_inproj(w_in[l], b_in[l]), cos, sa, sb, seq)
        cw = CMP_STRIDE * NSA_KV
        kcmp, vcmp = _compress(kc.reshape(bsz, ncb, cw), vc.reshape(bsz, ncb, cw),
                               *_pack_compress(cmp_pos[l], cmp_w1[l], cmp_b1[l], cmp_w2[l]))
        ocmp, member = _nsa_cmp(qn, kcmp, vcmp, ovt, bsz, seq)
        osel = _flash("sel", qnr, ks, vst, member, bsz, seq)
        owin = _flash("win", qnr, kw, vwt, member, bsz, seq)
        mmem = _moba_select(qb, _moba_kmean(kb, bsz, seq), bsz, seq)
        ob = _flash("moba", qb, kb, vbt, mmem, bsz, seq, tq=MOBA_TQ)
        x1, x1s = _post(ocmp, osel, owin, ob, gates, ma, mb, x2, eg,
                        w_proj_a[l].astype(BF16), w_proj_b[l].astype(BF16), w_out[l].astype(BF16),
                        ln1_g[l][None, :], ln1_b[l][None, :], alpha)
        top_idx, gate, rank, cnt = _router(x1, router_w[l].T.astype(BF16), router_bias[l][:, None])
        counts = cnt[:, 0].astype(jnp.int32)
        pad_start, blk_expert, n_used, nblk = _block_plan(counts, n)
        dest_tiles = _dest_rows(top_idx, rank, pad_start)
        xs = _dispatch(counts, pad_start, n_used, dest_tiles, x1s, nblk)
        y_slab = _experts(blk_expert, n_used, xs, w_gate, w_up, w_down, l)
        x2 = _combine(dest_tiles, y_slab, gate, x1, ws_gate[l].astype(BF16), ws_up[l].astype(BF16),
                      ws_down[l].astype(BF16), ln2_g[l][None, :], ln2_b[l][None, :], alpha)
    return x2.reshape(bsz, seq, d)
```

```python
import functools

import numpy as np
import jax
import jax.numpy as jnp
from jax import lax
from jax.experimental import pallas as pl
from jax.experimental.pallas import tpu as pltpu

D_MODEL = 1024
HEAD_DIM = 64
NSA_HEADS = 8
NSA_KV_HEADS = 2
MOBA_HEADS = 8
ROPE_THETA = 10000.0
CMP_BLOCK = 32
CMP_STRIDE = 16
CMP_HIDDEN = 128
SEL_BLOCK = 64
SEL_TOPN = 16
WINDOW = 512
MOBA_BLOCK = 256
MOBA_TOPK = 3
N_EXPERTS = 256
TOP_K = 8
N_GROUPS = 8
TOPK_GROUPS = 4
D_EXPERT = 256
ROUTED_SCALE = 2.5
LN_EPS = 1e-5
NEG = -1e30
BIG = 1e30
TAKEN = -3e38

NSA_Q = NSA_HEADS * HEAD_DIM
NSA_KV = NSA_KV_HEADS * HEAD_DIM
MOBA_W = MOBA_HEADS * HEAD_DIM
IN_SPLITS = (NSA_Q, NSA_KV, NSA_KV, NSA_KV, NSA_KV, NSA_KV, NSA_KV, 3 * NSA_HEADS,
             MOBA_W, MOBA_W, MOBA_W, D_MODEL, D_MODEL)

LANES = 128
SUBLANES = 8
SLAB = D_MODEL // (2 * LANES)
MOBA_SLOTS = LANES // MOBA_HEADS
ROW_BLOCK = 256
DISPATCH_TILE = 128
FLASH_TQ = 512
FLASH_TK = 512
MOBA_TQ = 1024
ONES_ROWS = 16
LOG2E = 1.4426950408889634
VMEM_LIMIT = 52 * 1024 * 1024

F32 = jnp.float32
BF16 = jnp.bfloat16
U32 = jnp.uint32
NT = (((1,), (1,)), ((), ()))


def _cp(sem, vmem=VMEM_LIMIT):
    return pltpu.CompilerParams(dimension_semantics=sem, vmem_limit_bytes=vmem)


def _iota(shape, axis):
    return lax.broadcasted_iota(jnp.int32, shape, axis)


def _inproj_layout():
    offs = np.concatenate([[0], np.cumsum(IN_SPLITS)])
    groups = []
    pieces = []

    def src(i):
        return np.arange(offs[i], offs[i + 1])

    def dup(i):
        c = src(i)
        return np.concatenate([c[:64], c[:64], c[64:], c[64:]])

    col = 0

    def add(cols, outs):
        nonlocal col
        pieces.append(cols)
        groups.append((col, len(cols), outs))
        col += len(cols)

    add(src(0), [(False, BF16), (True, BF16)])
    add(src(1), [(False, F32)])
    add(src(2), [(False, F32)])
    add(dup(3), [(True, BF16)])
    add(dup(5), [(True, BF16)])
    add(np.concatenate([src(7), -np.ones(LANES - 3 * NSA_HEADS, np.int64)]), [(False, F32)])
    add(src(8), [(True, BF16)])
    add(src(9), [(True, BF16)])
    add(src(11), [(False, F32)])
    add(src(12), [(False, F32)])
    tgroups, tpieces, row = [], [], 0
    for cols in (dup(4), dup(6), src(10)):
        tgroups.append((row, len(cols)))
        tpieces.append(cols)
        row += len(cols)
    return groups, np.concatenate(pieces), tgroups, np.concatenate(tpieces)


_INPROJ_GROUPS, _INPROJ_COLS, _INPROJ_TGROUPS, _INPROJ_TCOLS = _inproj_layout()


def _inproj_kernel(x_ref, w_ref, b_ref, wt_ref, bt_ref, cos_ref, sa_ref, sb_ref, *out_refs):
    xb = x_ref[...].astype(BF16)
    n_tok_major = sum(len(outs) for _, _, outs in _INPROJ_GROUPS)
    for (r0, rows), o_ref in zip(_INPROJ_TGROUPS, out_refs[n_tok_major:]):
        vt = lax.dot_general(wt_ref[r0:r0 + rows, :], xb, NT, preferred_element_type=F32) + bt_ref[r0:r0 + rows, :]
        o_ref[...] = vt.astype(BF16)
    oi = 0
    for c0, width, outs in _INPROJ_GROUPS:
        acc = jnp.dot(xb, w_ref[:, c0:c0 + width], preferred_element_type=F32) + b_ref[:, c0:c0 + width]
        for rope, dtype in outs:
            o_ref = out_refs[oi]
            oi += 1
            if not rope:
                o_ref[...] = acc.astype(dtype)
                continue
            cos = cos_ref[...]
            sa = sa_ref[...]
            sb = sb_ref[...]
            for j in range(0, width, LANES):
                t = acc[:, j:j + LANES]
                r = t * cos + pltpu.roll(t, LANES - HEAD_DIM // 2, 1) * sa + pltpu.roll(t, HEAD_DIM // 2, 1) * sb
                o_ref[:, j:j + LANES] = r.astype(dtype)


def _inproj(x2, w, b, wt, bt, cos, sa, sb, seq, tm=256):
    n = x2.shape[0]
    ctot = w.shape[1]
    rtot = wt.shape[0]
    nt_seq = seq // tm
    out_shape, out_specs = [], []
    for _, width, outs in _INPROJ_GROUPS:
        for _, dtype in outs:
            out_shape.append(jax.ShapeDtypeStruct((n, width), dtype))
            out_specs.append(pl.BlockSpec((tm, width), lambda i: (i, 0)))
    for _, rows in _INPROJ_TGROUPS:
        out_shape.append(jax.ShapeDtypeStruct((rows, n), BF16))
        out_specs.append(pl.BlockSpec((rows, tm), lambda i: (0, i)))
    tab = pl.BlockSpec((tm, LANES), lambda i: (i % nt_seq, 0))
    return pl.pallas_call(
        _inproj_kernel,
        grid=(n // tm,),
        in_specs=[pl.BlockSpec((tm, D_MODEL), lambda i: (i, 0)),
                  pl.BlockSpec((D_MODEL, ctot), lambda i: (0, 0)),
                  pl.BlockSpec((1, ctot), lambda i: (0, 0)),
                  pl.BlockSpec((rtot, D_MODEL), lambda i: (0, 0)),
                  pl.BlockSpec((rtot, 1), lambda i: (0, 0)),
                  tab, tab, tab],
        out_specs=out_specs,
        out_shape=out_shape,
        compiler_params=_cp(("parallel",)),
        name="inproj",
    )(x2, w, b, wt, bt, cos, sa, sb)


def _compress_kernel(kc_ref, vc_ref, ptop_ref, pbot_ref, wtop_ref, wbot_ref, b1_ref, w2_ref, ko_ref, vo_ref):
    ncb = kc_ref.shape[1]
    for kv, (c_ref, o_ref) in enumerate(((kc_ref, ko_ref), (vc_ref, vo_ref))):
        c = c_ref[0]
        a = jnp.dot((c + ptop_ref[kv]).astype(BF16), wtop_ref[kv], preferred_element_type=F32)
        bm = jnp.dot((c + pbot_ref[kv]).astype(BF16), wbot_ref[kv], preferred_element_type=F32)
        hid = jax.nn.gelu(a + pltpu.roll(bm, ncb - 1, 0) + b1_ref[kv])
        for g in range(NSA_KV_HEADS):
            hg = hid[:, g * CMP_HIDDEN:(g + 1) * CMP_HIDDEN].astype(BF16)
            o_ref[0, g] = jnp.dot(hg, w2_ref[kv], preferred_element_type=F32).astype(BF16)


def _compress(kc3, vc3, ptop, pbot, wtop, wbot, b1, w2):
    bsz, ncb, cw = kc3.shape
    hw = NSA_KV_HEADS * CMP_HIDDEN
    full = lambda shp: pl.BlockSpec(shp, lambda b: (0,) * len(shp))
    out = jax.ShapeDtypeStruct((bsz, NSA_KV_HEADS, ncb, LANES), BF16)
    ospec = pl.BlockSpec((1, NSA_KV_HEADS, ncb, LANES), lambda b: (b, 0, 0, 0))
    return pl.pallas_call(
        _compress_kernel,
        grid=(bsz,),
        in_specs=[pl.BlockSpec((1, ncb, cw), lambda b: (b, 0, 0)),
                  pl.BlockSpec((1, ncb, cw), lambda b: (b, 0, 0)),
                  full((2, 1, cw)), full((2, 1, cw)),
                  full((2, cw, hw)), full((2, cw, hw)),
                  full((2, 1, hw)), full((2, CMP_HIDDEN, LANES))],
        out_specs=[ospec, ospec],
        out_shape=[out, out],
        compiler_params=_cp(("parallel",)),
        name="nsa_compress",
    )(kc3, vc3, ptop, pbot, wtop, wbot, b1, w2)


def _rank_rows(val, n_rows):
    rows = _iota(val.shape, 0)
    rank = jnp.zeros(val.shape, F32)
    for i in range(n_rows):
        vi = val[i:i + 1, :]
        ahead = (vi > val) | ((vi == val) & (rows > i))
        rank = rank + jnp.where(ahead, 1.0, 0.0)
    return rank


def _transpose_to_rows(mt):
    tq = mt.shape[1]
    return jnp.concatenate([mt[:, c:c + LANES].T for c in range(0, tq, LANES)], axis=0)


def _nsa_cmp_kernel(q_ref, kc_ref, vc_ref, ov_ref, o_ref, mem_ref, *, tq, n_cmp, n_sel):
    i = pl.program_id(2)
    ncb = kc_ref.shape[2]
    kc = kc_ref[0, 0]
    vc = vc_ref[0, 0]
    pos = i * tq + _iota((tq, ncb), 0)
    blk = _iota((tq, ncb), 1)
    valid = (blk * CMP_STRIDE + CMP_BLOCK - 1 <= pos) & (blk < n_cmp)
    anyv = (i * tq + _iota((tq, 1), 0) >= CMP_BLOCK - 1).astype(F32)
    lane = _iota((1, LANES), 1)
    scale = HEAD_DIM ** -0.5
    psum = jnp.zeros((tq, ncb), F32)
    outs = []
    hg = NSA_HEADS // NSA_KV_HEADS
    for t in range(hg):
        qb = q_ref[:, (t // 2) * LANES:(t // 2 + 1) * LANES]
        half = (lane < HEAD_DIM) if t % 2 == 0 else (lane >= HEAD_DIM)
        qm = jnp.where(half, qb, jnp.zeros_like(qb))
        s = lax.dot_general(qm, kc, NT, preferred_element_type=F32) * scale
        s = jnp.where(valid, s, NEG)
        e = jnp.exp(s - jnp.max(s, axis=1, keepdims=True))
        p = e / jnp.sum(e, axis=1, keepdims=True) * anyv
        psum = psum + p
        outs.append(jnp.dot(p.astype(BF16), vc, preferred_element_type=F32))
    for r in range(hg // 2):
        o_ref[:, r * LANES:(r + 1) * LANES] = jnp.where(lane < HEAD_DIM, outs[2 * r], outs[2 * r + 1]).astype(BF16)

    imp = lax.dot_general(ov_ref[...], psum, NT, preferred_element_type=F32, precision=lax.Precision.HIGHEST)
    imp = imp[:n_sel]
    j = _iota((n_sel, tq), 0)
    cur = (i * tq + _iota((n_sel, tq), 1)) // SEL_BLOCK
    forced = (j == 0) | (j == cur) | (j == cur - 1)
    val = jnp.where(forced, BIG, jnp.where(j <= cur, imp, NEG))
    rank = _rank_rows(val, n_sel)
    member = jnp.where((rank < float(min(SEL_TOPN, n_sel))) & (j <= cur), 1.0, 0.0)
    if n_sel < LANES:
        member = jnp.concatenate([member, jnp.zeros((LANES - n_sel, tq), F32)], axis=0)
    mem_ref[0, 0] = member


def _nsa_cmp(qn, kcmp, vcmp, ovt, bsz, seq, tq=256):
    n = qn.shape[0]
    nq = seq // tq
    ncb = kcmp.shape[2]
    n_sel = seq // SEL_BLOCK
    gw = NSA_Q // NSA_KV_HEADS
    kern = functools.partial(_nsa_cmp_kernel, tq=tq, n_cmp=ncb - 1, n_sel=n_sel)
    kvspec = pl.BlockSpec((1, 1, ncb, LANES), lambda b, g, i: (b, g, 0, 0))
    return pl.pallas_call(
        kern,
        grid=(bsz, NSA_KV_HEADS, nq),
        in_specs=[pl.BlockSpec((tq, gw), lambda b, g, i: (b * nq + i, g)),
                  kvspec, kvspec,
                  pl.BlockSpec((LANES, ncb), lambda b, g, i: (0, 0))],
        out_specs=[pl.BlockSpec((tq, gw), lambda b, g, i: (b * nq + i, g)),
                   pl.BlockSpec((1, 1, LANES, tq), lambda b, g, i: (b, g, 0, i))],
        out_shape=[jax.ShapeDtypeStruct((n, NSA_Q), BF16),
                   jax.ShapeDtypeStruct((bsz, NSA_KV_HEADS, LANES, seq), F32)],
        compiler_params=_cp(("parallel", "parallel", "parallel")),
        name="nsa_cmp_select",
    )(qn, kcmp, vcmp, ovt)


def _moba_kmean_kernel(k_ref, o_ref):
    nblk = o_ref.shape[1]
    k = k_ref[...].astype(F32).reshape(nblk, MOBA_BLOCK, MOBA_W)
    o_ref[0] = jnp.sum(k, axis=1) * (1.0 / MOBA_BLOCK)


def _moba_kmean(kb, bsz, seq):
    nblk = seq // MOBA_BLOCK
    return pl.pallas_call(
        _moba_kmean_kernel,
        grid=(bsz,),
        in_specs=[pl.BlockSpec((seq, MOBA_W), lambda b: (b, 0))],
        out_specs=pl.BlockSpec((1, nblk, MOBA_W), lambda b: (b, 0, 0)),
        out_shape=jax.ShapeDtypeStruct((bsz, nblk, MOBA_W), F32),
        compiler_params=_cp(("parallel",)),
        name="moba_kmean",
    )(kb)


def _moba_select_kernel(q_ref, km_ref, mem_ref, *, tq, nblk):
    i = pl.program_id(1)
    km = km_ref[0]
    if nblk < MOBA_SLOTS:
        km = jnp.concatenate([km, jnp.zeros((MOBA_SLOTS - nblk, MOBA_W), F32)], axis=0)
    kmt = jnp.concatenate([km] * MOBA_HEADS, axis=0)
    r = _iota((LANES, MOBA_W), 0)
    c = _iota((LANES, MOBA_W), 1)
    kmt = jnp.where(r // MOBA_SLOTS == c // HEAD_DIM, kmt, 0.0).astype(BF16)
    gate = lax.dot_general(kmt, q_ref[...], NT, preferred_element_type=F32)
    members = []
    for h in range(MOBA_HEADS):
        sub = gate[h * MOBA_SLOTS:(h + 1) * MOBA_SLOTS]
        j = _iota((MOBA_SLOTS, tq), 0)
        cur = (i * tq + _iota((MOBA_SLOTS, tq), 1)) // MOBA_BLOCK
        past = j < cur
        val = jnp.where(past, sub, NEG)
        rank = _rank_rows(val, min(nblk, MOBA_SLOTS))
        members.append(jnp.where(((rank < float(MOBA_TOPK)) & past) | (j == cur), 1.0, 0.0))
    mem_ref[...] = jnp.concatenate(members, axis=0)


def _moba_select(qb, kmean, bsz, seq, tq=256):
    n = qb.shape[0]
    nq = seq // tq
    nblk = seq // MOBA_BLOCK
    kern = functools.partial(_moba_select_kernel, tq=tq, nblk=nblk)
    return pl.pallas_call(
        kern,
        grid=(bsz, nq),
        in_specs=[pl.BlockSpec((tq, MOBA_W), lambda b, i: (b * nq + i, 0)),
                  pl.BlockSpec((1, nblk, MOBA_W), lambda b, i: (b, 0, 0))],
        out_specs=pl.BlockSpec((LANES, tq), lambda b, i: (0, b * nq + i)),
        out_shape=jax.ShapeDtypeStruct((LANES, n), F32),
        compiler_params=_cp(("parallel", "parallel")),
        name="moba_select",
    )(qb, kmean)


def _flash_kernel(qi_ref, kj_ref, first_ref, last_ref, edge_ref, q_ref, k_ref, vt_ref, mem_ref, o_ref,
                  m_sc, acc_sc, *, mode, nhq, tq, tk):
    hg = pl.program_id(1)
    p = pl.program_id(2)
    i = qi_ref[p]
    jj = kj_ref[p]

    @pl.when(first_ref[p] == 1)
    def _():
        m_sc[...] = jnp.full(m_sc.shape, NEG, F32)
        acc_sc[...] = jnp.zeros(acc_sc.shape, F32)

    def member_rows(first_row, keys_per_row):
        rows = [mem_ref[pl.ds(first_row + r, 1), :] > 0.5 for r in range(tk // keys_per_row)]
        return jnp.concatenate([jnp.broadcast_to(r, (keys_per_row, tq)) for r in rows], axis=0)

    def step(positional):
        allowed = None
        if positional:
            kpos = jj * tk + _iota((tk, tq), 0)
            qpos = i * tq + _iota((tk, tq), 1)
            allowed = kpos <= qpos
            if mode == "win":
                allowed = allowed & (kpos > qpos - WINDOW)
        if mode == "sel":
            memb = member_rows(jj * (tk // SEL_BLOCK), SEL_BLOCK)
            allowed = memb if allowed is None else allowed & memb
        k = k_ref[...]
        vt1 = jnp.concatenate([vt_ref[...], jnp.ones((ONES_ROWS, tk), BF16)], axis=0)
        lane = _iota((1, LANES), 1)
        qms = []
        for t in range(nhq):
            qb = q_ref[:, (t // 2) * LANES:(t // 2 + 1) * LANES].astype(F32)
            half = (lane < HEAD_DIM) if t % 2 == 0 else (lane >= HEAD_DIM)
            qms.append(jnp.where(half, qb * (HEAD_DIM ** -0.5 * LOG2E), 0.0).astype(BF16))
        s_all = lax.dot_general(k, jnp.concatenate(qms, axis=0), NT, preferred_element_type=F32)
        prs, alphas = [], []
        for t in range(nhq):
            s = s_all[:, t * tq:(t + 1) * tq]
            ok = allowed
            if mode == "moba":
                memb = member_rows((hg * 2 + t) * MOBA_SLOTS + jj * (tk // MOBA_BLOCK), MOBA_BLOCK)
                ok = memb if ok is None else ok & memb
            if ok is not None:
                s = jnp.where(ok, s, NEG)
            m_old = m_sc[:, t * tq:(t + 1) * tq]
            m_new = jnp.maximum(m_old, jnp.max(s, axis=0, keepdims=True))
            prs.append(jnp.exp2(s - m_new).astype(BF16))
            alphas.append(jnp.exp2(m_old - m_new))
            m_sc[:, t * tq:(t + 1) * tq] = m_new
        pv = jnp.dot(vt1, jnp.concatenate(prs, axis=1), preferred_element_type=F32)
        acc_sc[...] = jnp.concatenate(alphas, axis=1) * acc_sc[...] + pv

    @pl.when(edge_ref[p] == 1)
    def _():
        step(True)

    @pl.when(edge_ref[p] == 0)
    def _():
        step(False)

    @pl.when(last_ref[p] == 1)
    def _():
        for r in range(nhq // 2):
            a0 = acc_sc[:, 2 * r * tq:(2 * r + 1) * tq]
            a1 = acc_sc[:, (2 * r + 1) * tq:(2 * r + 2) * tq]
            o0 = a0[:HEAD_DIM] / a0[LANES:LANES + 1]
            o1 = a1[HEAD_DIM:LANES] / a1[LANES:LANES + 1]
            ot = jnp.concatenate([o0, o1], axis=0)
            o_ref[:, r * LANES:(r + 1) * LANES] = _transpose_to_rows(ot).astype(BF16)


def _pair_tables(nq, tq, tk, window):
    qi, kj, first, last, edge = [], [], [], [], []
    for i in range(nq):
        q_lo, q_hi = i * tq, i * tq + tq - 1
        lo = 0 if window is None else max(0, (q_lo - window + 1) // tk)
        hi = q_hi // tk
        for j in range(lo, hi + 1):
            k_lo, k_hi = j * tk, j * tk + tk - 1
            inside = k_hi <= q_lo and (window is None or k_lo > q_hi - window)
            qi.append(i)
            kj.append(j)
            first.append(int(j == lo))
            last.append(int(j == hi))
            edge.append(int(not inside))
    mk = lambda a: jnp.asarray(np.asarray(a, np.int32))
    return mk(qi), mk(kj), mk(first), mk(last), mk(edge)


def _flash(mode, q, k, vt, mem, bsz, seq, tq=FLASH_TQ, tk=FLASH_TK):
    assert tk % MOBA_BLOCK == 0 and tk % SEL_BLOCK == 0 and seq % tq == 0 and seq % tk == 0
    n = q.shape[0]
    nq = seq // tq
    nk = seq // tk
    if mode == "moba":
        ngrp, nhq = MOBA_HEADS // 2, 2
        mem_spec = pl.BlockSpec((LANES, tq), lambda b, g, p, qi, kj, fi, la, ed: (0, b * nq + qi[p]))
    else:
        ngrp, nhq = NSA_KV_HEADS, NSA_HEADS // NSA_KV_HEADS
        mem_spec = pl.BlockSpec((None, None, LANES, tq), lambda b, g, p, qi, kj, fi, la, ed: (b, g, 0, qi[p]))
    tables = _pair_tables(nq, tq, tk, WINDOW if mode == "win" else None)
    npairs = int(tables[0].shape[0])
    qw = nhq // 2 * LANES
    kern = functools.partial(_flash_kernel, mode=mode, nhq=nhq, tq=tq, tk=tk)
    kspec = pl.BlockSpec((tk, LANES), lambda b, g, p, qi, kj, fi, la, ed: (b * nk + kj[p], g))
    vtspec = pl.BlockSpec((LANES, tk), lambda b, g, p, qi, kj, fi, la, ed: (g, b * nk + kj[p]))
    qspec = pl.BlockSpec((tq, qw), lambda b, g, p, qi, kj, fi, la, ed: (b * nq + qi[p], g))
    grid_spec = pltpu.PrefetchScalarGridSpec(
        num_scalar_prefetch=5,
        grid=(bsz, ngrp, npairs),
        in_specs=[qspec, kspec, vtspec, mem_spec],
        out_specs=qspec,
        scratch_shapes=[pltpu.VMEM((1, nhq * tq), F32),
                        pltpu.VMEM((LANES + ONES_ROWS, nhq * tq), F32)],
    )
    return pl.pallas_call(
        kern,
        grid_spec=grid_spec,
        out_shape=jax.ShapeDtypeStruct((n, ngrp * qw), BF16),
        compiler_params=_cp(("parallel", "parallel", "arbitrary")),
        name="flash_" + mode,
    )(*tables, q, k, vt, mem)


def _layer_norm(z, g, b):
    mu = jnp.mean(z, axis=-1, keepdims=True)
    zc = z - mu
    var = jnp.mean(zc * zc, axis=-1, keepdims=True)
    return zc * lax.rsqrt(var + LN_EPS) * g + b


HIGH_HALF = 0xFFFF0000


def _unpack_words(w):
    lo = lax.bitcast_convert_type(w << 16, F32)
    hi = lax.bitcast_convert_type(w & jnp.uint32(HIGH_HALF), F32)
    return lo, hi


def _store_slabs(slab_ref, val, rows):
    half = D_MODEL // 2
    lo = lax.bitcast_convert_type(val[:, :half].astype(BF16).astype(F32), U32) >> 16
    hi = lax.bitcast_convert_type(val[:, half:].astype(BF16).astype(F32), U32) & jnp.uint32(HIGH_HALF)
    w = lo | hi
    for c in range(SLAB):
        slab_ref[pl.ds(c, rows, stride=SLAB), :] = w[:, c * LANES:(c + 1) * LANES]


def _post_kernel(ocmp_ref, osel_ref, owin_ref, ob_ref, g_ref, ma_ref, mb_ref, x_ref,
                 eg_ref, wpa_ref, wpb_ref, wout_ref, lng_ref, lnb_ref, x1_ref, x1s_ref, *, alpha, tm):
    sig = jax.nn.sigmoid(g_ref[...])
    oa = jnp.zeros((tm, NSA_Q), F32)
    for c, o_ref in enumerate((ocmp_ref, osel_ref, owin_ref)):
        gexp = jnp.dot(sig, eg_ref[c], preferred_element_type=F32, precision=lax.Precision.HIGHEST)
        oa = oa + gexp * o_ref[...].astype(F32)
    pa = jnp.dot(oa.astype(BF16), wpa_ref[...], preferred_element_type=F32)
    pb = jnp.dot(ob_ref[...], wpb_ref[...], preferred_element_type=F32)
    merged = jax.nn.sigmoid(ma_ref[...]) * pa + jax.nn.sigmoid(mb_ref[...]) * pb
    y = jnp.dot(merged.astype(BF16), wout_ref[...], preferred_element_type=F32)
    x1 = _layer_norm(alpha * x_ref[...] + y, lng_ref[...], lnb_ref[...])
    x1_ref[...] = x1
    _store_slabs(x1s_ref, x1, tm)


def _post(ocmp, osel, owin, ob, gates, ma, mb, x2, eg, wpa, wpb, wout, lng, lnb, alpha, tm=256):
    n = x2.shape[0]
    row = lambda w: pl.BlockSpec((tm, w), lambda i: (i, 0))
    full = lambda shp: pl.BlockSpec(shp, lambda i: (0,) * len(shp))
    kern = functools.partial(_post_kernel, alpha=alpha, tm=tm)
    return pl.pallas_call(
        kern,
        grid=(n // tm,),
        in_specs=[row(NSA_Q), row(NSA_Q), row(NSA_Q), row(MOBA_W), row(LANES), row(D_MODEL), row(D_MODEL),
                  row(D_MODEL), full((3, LANES, NSA_Q)), full((NSA_Q, D_MODEL)), full((MOBA_W, D_MODEL)),
                  full((D_MODEL, D_MODEL)), full((1, D_MODEL)), full((1, D_MODEL))],
        out_specs=[row(D_MODEL), pl.BlockSpec((tm * SLAB, LANES), lambda i: (i, 0))],
        out_shape=[jax.ShapeDtypeStruct((n, D_MODEL), F32), jax.ShapeDtypeStruct((n * SLAB, LANES), U32)],
        compiler_params=_cp(("parallel",)),
        name="post_attention",
    )(ocmp, osel, owin, ob, gates, ma, mb, x2, eg, wpa, wpb, wout, lng, lnb)


def _first_argmax_rows(cur, rows, n_rows):
    mx = jnp.max(cur, axis=0, keepdims=True)
    idx = jnp.min(jnp.where(cur == mx, rows, n_rows), axis=0, keepdims=True)
    return mx, idx


def _router_kernel(x_ref, rwt_ref, bias_ref, idx_ref, gate_ref, rank_ref, cnt_ref, carry_sc, *, tm):
    i = pl.program_id(0)

    @pl.when(i == 0)
    def _():
        carry_sc[...] = jnp.zeros(carry_sc.shape, F32)

    logits = lax.dot_general(rwt_ref[...], x_ref[...].astype(BF16), NT, preferred_element_type=F32)
    sc = jax.nn.sigmoid(logits)
    biased = sc + bias_ref[...]
    gsz = N_EXPERTS // N_GROUPS
    grow = _iota((gsz, tm), 0)
    gscore = []
    for g in range(N_GROUPS):
        grp = biased[g * gsz:(g + 1) * gsz]
        m1, i1 = _first_argmax_rows(grp, grow, gsz)
        m2 = jnp.max(jnp.where(grow == i1, TAKEN, grp), axis=0, keepdims=True)
        gscore.append(m1 + m2)
    gs = jnp.concatenate(gscore, axis=0)
    keep = _rank_rows(gs, N_GROUPS) < float(TOPK_GROUPS)
    cur = jnp.concatenate(
        [jnp.where(keep[g:g + 1], biased[g * gsz:(g + 1) * gsz], NEG) for g in range(N_GROUPS)], axis=0)
    rows = _iota((N_EXPERTS, tm), 0)
    sels, tops, idxs = [], [], []
    for _ in range(TOP_K):
        _, ik = _first_argmax_rows(cur, rows, N_EXPERTS)
        selk = rows == ik
        tops.append(jnp.sum(jnp.where(selk, sc, 0.0), axis=0, keepdims=True))
        cur = jnp.where(selk, TAKEN, cur)
        sels.append(selk)
        idxs.append(ik)
    denom = tops[0]
    for tk_ in tops[1:]:
        denom = denom + tk_
    selall = jnp.where(cur == TAKEN, 1.0, 0.0)
    upper = jnp.where(_iota((tm, tm), 0) < _iota((tm, tm), 1), 1.0, 0.0).astype(BF16)
    before = jnp.dot(selall.astype(BF16), upper, preferred_element_type=F32) + carry_sc[...]
    idx_ref[...] = jnp.concatenate(idxs, axis=0)
    gate_ref[...] = jnp.concatenate([tk_ / denom * ROUTED_SCALE for tk_ in tops], axis=0)
    rank_ref[...] = jnp.concatenate(
        [jnp.sum(jnp.where(s, before, 0.0), axis=0, keepdims=True) for s in sels], axis=0).astype(jnp.int32)
    carry_sc[...] = carry_sc[...] + jnp.sum(selall, axis=1, keepdims=True)
    cnt_ref[...] = jnp.broadcast_to(carry_sc[...], cnt_ref.shape)


def _router(x1, rwt, bias, tm=256):
    n = x1.shape[0]
    kspec = pl.BlockSpec((TOP_K, tm), lambda i: (0, i))
    return pl.pallas_call(
        functools.partial(_router_kernel, tm=tm),
        grid=(n // tm,),
        in_specs=[pl.BlockSpec((tm, D_MODEL), lambda i: (i, 0)),
                  pl.BlockSpec((N_EXPERTS, D_MODEL), lambda i: (0, 0)),
                  pl.BlockSpec((N_EXPERTS, 1), lambda i: (0, 0))],
        out_specs=[kspec, kspec, kspec, pl.BlockSpec((N_EXPERTS, LANES), lambda i: (0, 0))],
        out_shape=[jax.ShapeDtypeStruct((TOP_K, n), jnp.int32), jax.ShapeDtypeStruct((TOP_K, n), F32),
                   jax.ShapeDtypeStruct((TOP_K, n), jnp.int32), jax.ShapeDtypeStruct((N_EXPERTS, LANES), F32)],
        scratch_shapes=[pltpu.VMEM((N_EXPERTS, 1), F32)],
        compiler_params=_cp(("arbitrary",)),
        name="router",
    )(x1, rwt, bias)


def _dest_kernel(idx_ref, rank_ref, pstart_ref, o_ref, *, tm):
    rows = _iota((N_EXPERTS, tm), 0)
    pstart = pstart_ref[...]
    for k in range(TOP_K):
        hit = rows == idx_ref[k:k + 1, :]
        base = jnp.sum(jnp.where(hit, pstart, 0.0), axis=0, keepdims=True)
        o_ref[:, k * tm:(k + 1) * tm] = base.astype(jnp.int32) + rank_ref[k:k + 1, :]


def _dest_rows(top_idx, rank, pad_start, tm=DISPATCH_TILE):
    n = top_idx.shape[1]
    kspec = pl.BlockSpec((TOP_K, tm), lambda i: (0, i))
    out = pl.pallas_call(
        functools.partial(_dest_kernel, tm=tm),
        grid=(n // tm,),
        in_specs=[kspec, kspec, pl.BlockSpec((N_EXPERTS, 1), lambda i: (0, 0))],
        out_specs=pl.BlockSpec((None, 1, TOP_K * tm), lambda i: (i, 0, 0)),
        out_shape=jax.ShapeDtypeStruct((n // tm, 1, TOP_K * tm), jnp.int32),
        compiler_params=_cp(("parallel",)),
        name="moe_dest_rows",
    )(top_idx, rank, pad_start.astype(F32)[:, None])
    return out.reshape(n // tm, TOP_K, tm).transpose(0, 2, 1).reshape(n // tm, tm * TOP_K)


def _issue_per_token(tm, issue):
    def body(t, carry):
        for k in range(TOP_K):
            issue(t, k)
        return carry
    lax.fori_loop(0, tm, body, 0)


def _dispatch_kernel(cnt_ref, pstart_ref, nused_ref, dest_hbm, x_ref, xs_hbm, idx_sm, zero_sc, idx_sem, row_sem,
                     zero_sem, *, ntile, tm, nblk):
    i = pl.program_id(0)
    slot = i % 2
    nrow = TOP_K * tm

    def idx_copy(t, s):
        return pltpu.make_async_copy(dest_hbm.at[t], idx_sm.at[s], idx_sem.at[s])

    def zero_copy(row, rows):
        return pltpu.make_async_copy(zero_sc.at[pl.ds(0, rows * SLAB)],
                                     xs_hbm.at[pl.ds(pl.multiple_of(row * SLAB, SLAB), rows * SLAB)], zero_sem.at[0])

    def zero_fill(act):
        def per_expert(e, carry):
            cnt = cnt_ref[e]
            npad = (ROW_BLOCK - cnt % ROW_BLOCK) % ROW_BLOCK
            row = pstart_ref[e] + cnt
            piece = ROW_BLOCK // 2
            while piece >= 1:
                @pl.when((npad & piece) != 0)
                def _(row=row, piece=piece):
                    act(zero_copy(row, piece))
                row = row + (npad & piece)
                piece //= 2
            return carry
        lax.fori_loop(0, N_EXPERTS, per_expert, 0)

        def per_block(bk, carry):
            act(zero_copy(bk * ROW_BLOCK, ROW_BLOCK))
            return carry
        lax.fori_loop(nused_ref[0], nblk, per_block, 0)

    @pl.when(i == 0)
    def _():
        idx_copy(0, 0).start()
        if ntile > 1:
            idx_copy(1, 1).start()
        zero_sc[...] = jnp.zeros(zero_sc.shape, U32)
        zero_fill(lambda cp: cp.start())
        zero_fill(lambda cp: cp.wait())

    idx_copy(i, slot).wait()

    def issue(t, k):
        src = x_ref.at[pl.ds(pl.multiple_of(t * SLAB, SLAB), SLAB)]
        dst = xs_hbm.at[pl.ds(pl.multiple_of(idx_sm[slot, t * TOP_K + k] * SLAB, SLAB), SLAB)]
        pltpu.make_async_copy(src, dst, row_sem.at[0]).start()
    _issue_per_token(tm, issue)

    @pl.when(i + 2 < ntile)
    def _():
        idx_copy(i + 2, slot).start()

    for _ in range(TOP_K):
        pltpu.make_async_copy(x_ref, xs_hbm.at[pl.ds(0, tm * SLAB)], row_sem.at[0]).wait()


def _dispatch(counts, pad_start, n_used, dest_tiles, x1s, nblk, tm=DISPATCH_TILE):
    ntile = dest_tiles.shape[0]
    grid_spec = pltpu.PrefetchScalarGridSpec(
        num_scalar_prefetch=3,
        grid=(ntile,),
        in_specs=[pl.BlockSpec(memory_space=pl.ANY),
                  pl.BlockSpec((tm * SLAB, LANES), lambda i, cn, ps, nu: (i, 0))],
        out_specs=pl.BlockSpec(memory_space=pl.ANY),
        scratch_shapes=[pltpu.SMEM((2, TOP_K * tm), jnp.int32),
                        pltpu.VMEM((ROW_BLOCK * SLAB, LANES), U32),
                        pltpu.SemaphoreType.DMA((2,)),
                        pltpu.SemaphoreType.DMA((1,)),
                        pltpu.SemaphoreType.DMA((1,))],
    )
    return pl.pallas_call(
        functools.partial(_dispatch_kernel, ntile=ntile, tm=tm, nblk=nblk),
        grid_spec=grid_spec,
        out_shape=jax.ShapeDtypeStruct((nblk * ROW_BLOCK * SLAB, LANES), U32),
        compiler_params=_cp(("arbitrary",)),
        name="moe_dispatch",
    )(counts, pad_start, n_used, dest_tiles, x1s)


def _expert_kernel(be_ref, nused_ref, xs_ref, wg_ref, wu_ref, wd_ref, y_ref, xmat, wgu_sc, wd_sc):
    i = pl.program_id(0)
    rb = ROW_BLOCK

    @pl.when(i < nused_ref[0])
    def _():
        e = be_ref[i]
        prev = be_ref[jnp.maximum(i - 1, 0)]

        @pl.when((i == 0) | (e != prev))
        def _():
            wgu_sc[:, :D_EXPERT] = wg_ref[0, 0].astype(BF16)
            wgu_sc[:, D_EXPERT:] = wu_ref[0, 0].astype(BF16)
            wd_sc[...] = wd_ref[0, 0].astype(BF16)

        half = D_MODEL // 2
        for c in range(SLAB):
            lo, hi = _unpack_words(xs_ref[pl.ds(c, rb, stride=SLAB), :])
            xmat[:, c * LANES:(c + 1) * LANES] = lo.astype(BF16)
            xmat[:, half + c * LANES:half + (c + 1) * LANES] = hi.astype(BF16)
        hgu = jnp.dot(xmat[...], wgu_sc[...], preferred_element_type=F32)
        h = jax.nn.silu(hgu[:, :D_EXPERT]) * hgu[:, D_EXPERT:]
        y = jnp.dot(h.astype(BF16), wd_sc[...], preferred_element_type=F32)
        _store_slabs(y_ref, y, rb)

    @pl.when(i >= nused_ref[0])
    def _():
        y_ref[...] = jnp.zeros(y_ref.shape, U32)


def _experts(blk_expert, n_used, xs, w_gate, w_up, w_down, layer):
    nblk = blk_expert.shape[0]
    rb = ROW_BLOCK
    blk = lambda i, nu: jnp.minimum(i, nu[0] - 1)
    wspec = lambda shp: pl.BlockSpec((1, 1) + shp, lambda i, be, nu: (layer, be[blk(i, nu)], 0, 0))
    grid_spec = pltpu.PrefetchScalarGridSpec(
        num_scalar_prefetch=2,
        grid=(nblk,),
        in_specs=[pl.BlockSpec((rb * SLAB, LANES), lambda i, be, nu: (blk(i, nu), 0)),
                  wspec((D_MODEL, D_EXPERT)), wspec((D_MODEL, D_EXPERT)), wspec((D_EXPERT, D_MODEL))],
        out_specs=pl.BlockSpec((rb * SLAB, LANES), lambda i, be, nu: (i, 0)),
        scratch_shapes=[pltpu.VMEM((rb, D_MODEL), BF16),
                        pltpu.VMEM((D_MODEL, 2 * D_EXPERT), BF16),
                        pltpu.VMEM((D_EXPERT, D_MODEL), BF16)],
    )
    return pl.pallas_call(
        _expert_kernel,
        grid_spec=grid_spec,
        out_shape=jax.ShapeDtypeStruct((nblk * rb * SLAB, LANES), U32),
        compiler_params=_cp(("arbitrary",)),
        name="moe_experts",
    )(blk_expert, n_used, xs, w_gate, w_up, w_down)


def _combine_kernel(dest_hbm, y_hbm, gate_ref, x1_ref, wsg_ref, wsu_ref, wsd_ref, lng_ref, lnb_ref, o_ref,
                    idx_sm, ybuf, gbuf, acc_sc, idx_sem, row_sem, *, ntile, tm, alpha):
    i = pl.program_id(0)
    slot = i % 2
    nrow = TOP_K * tm

    def idx_copy(t, s):
        return pltpu.make_async_copy(dest_hbm.at[t], idx_sm.at[s], idx_sem.at[s])

    def gather(s):
        def issue(t, k):
            src = y_hbm.at[pl.ds(pl.multiple_of(idx_sm[s, t * TOP_K + k] * SLAB, SLAB), SLAB)]
            dst = ybuf.at[s, pl.ds(pl.multiple_of((k * tm + t) * SLAB, SLAB), SLAB)]
            pltpu.make_async_copy(src, dst, row_sem.at[s]).start()
        _issue_per_token(tm, issue)

    @pl.when(i == 0)
    def _():
        idx_copy(0, 0).start()
        idx_copy(0, 0).wait()
        gather(0)
        if ntile > 1:
            idx_copy(1, 1).start()

    @pl.when(i + 1 < ntile)
    def _():
        idx_copy(i + 1, 1 - slot).wait()
        gather(1 - slot)

    @pl.when(i + 2 < ntile)
    def _():
        idx_copy(i + 2, slot).start()

    x1 = x1_ref[...]
    xb = x1.astype(BF16)
    hs = jax.nn.silu(jnp.dot(xb, wsg_ref[...], preferred_element_type=F32)) * jnp.dot(
        xb, wsu_ref[...], preferred_element_type=F32)
    shared = jnp.dot(hs.astype(BF16), wsd_ref[...], preferred_element_type=F32)

    for k in range(TOP_K):
        gcol = _transpose_to_rows(jnp.broadcast_to(gate_ref[k:k + 1, :], (LANES, tm)))
        for c in range(SLAB):
            gbuf[pl.ds(k * tm * SLAB + c, tm, stride=SLAB), :] = gcol

    pltpu.make_async_copy(y_hbm.at[pl.ds(0, nrow * SLAB)], ybuf.at[slot], row_sem.at[slot]).wait()
    acc_lo = jnp.zeros((tm * SLAB, LANES), F32)
    acc_hi = jnp.zeros((tm * SLAB, LANES), F32)
    for k in range(TOP_K):
        lo, hi = _unpack_words(ybuf[slot, pl.ds(k * tm * SLAB, tm * SLAB), :])
        g = gbuf[pl.ds(k * tm * SLAB, tm * SLAB), :]
        acc_lo = acc_lo + lo * g
        acc_hi = acc_hi + hi * g
    acc_sc[0] = acc_lo
    acc_sc[1] = acc_hi
    routed = jnp.concatenate([acc_sc[h, pl.ds(c, tm, stride=SLAB), :] for h in range(2) for c in range(SLAB)],
                             axis=1)
    o_ref[...] = _layer_norm(alpha * x1 + routed + shared, lng_ref[...], lnb_ref[...])


def _combine(dest_tiles, y_slab, gate, x1, wsg, wsu, wsd, lng, lnb, alpha, tm=DISPATCH_TILE):
    n = x1.shape[0]
    ntile = n // tm
    nrow = TOP_K * tm
    full = lambda shp: pl.BlockSpec(shp, lambda i: (0,) * len(shp))
    kern = functools.partial(_combine_kernel, ntile=ntile, tm=tm, alpha=alpha)
    return pl.pallas_call(
        kern,
        grid=(ntile,),
        in_specs=[pl.BlockSpec(memory_space=pl.ANY),
                  pl.BlockSpec(memory_space=pl.ANY),
                  pl.BlockSpec((TOP_K, tm), lambda i: (0, i)),
                  pl.BlockSpec((tm, D_MODEL), lambda i: (i, 0)),
                  full((D_MODEL, D_EXPERT)), full((D_MODEL, D_EXPERT)), full((D_EXPERT, D_MODEL)),
                  full((1, D_MODEL)), full((1, D_MODEL))],
        out_specs=pl.BlockSpec((tm, D_MODEL), lambda i: (i, 0)),
        out_shape=jax.ShapeDtypeStruct((n, D_MODEL), F32),
        scratch_shapes=[pltpu.SMEM((2, nrow), jnp.int32),
                        pltpu.VMEM((2, nrow * SLAB, LANES), U32),
                        pltpu.VMEM((nrow * SLAB, LANES), F32),
                        pltpu.VMEM((2, tm * SLAB, LANES), F32),
                        pltpu.SemaphoreType.DMA((2,)),
                        pltpu.SemaphoreType.DMA((2,))],
        compiler_params=_cp(("arbitrary",)),
        name="moe_combine",
    )(dest_tiles, y_slab, gate, x1, wsg, wsu, wsd, lng, lnb)


def _rope_tables(seq):
    pos = jnp.arange(seq, dtype=F32)
    inv = ROPE_THETA ** (-jnp.arange(0, HEAD_DIM, 2, dtype=F32) / HEAD_DIM)
    ang = pos[:, None] * inv[None, :]
    ang = jnp.concatenate([ang] * (LANES // (HEAD_DIM // 2)), axis=-1)
    first = (np.arange(LANES) % HEAD_DIM) < HEAD_DIM // 2
    sin = jnp.sin(ang)
    return jnp.cos(ang), jnp.where(first, -sin, 0.0), jnp.where(first, 0.0, sin)


def _pack_inproj(w_in, b_in):
    cols = _INPROJ_COLS
    safe = np.where(cols < 0, 0, cols)
    live = jnp.asarray(cols >= 0)
    w = jnp.where(live[None, :], w_in[:, safe], 0.0).astype(BF16)
    b = jnp.where(live, b_in[safe], 0.0)[None, :]
    wt = w_in[:, _INPROJ_TCOLS].T.astype(BF16)
    bt = b_in[_INPROJ_TCOLS][:, None]
    return w, b, wt, bt


def _pack_compress(cmp_pos, cmp_w1, cmp_b1, cmp_w2):
    half = CMP_BLOCK // 2
    g = NSA_KV_HEADS

    def w_half(w1, lo):
        w = w1.reshape(2, CMP_BLOCK, HEAD_DIM, CMP_HIDDEN)[:, lo:lo + half]
        eye = jnp.eye(g, dtype=F32)
        w = jnp.einsum("klde,gh->klgdhe", w, eye)
        return w.reshape(2, half * g * HEAD_DIM, g * CMP_HIDDEN).astype(BF16)

    def p_half(lo):
        p = cmp_pos[:, lo:lo + half]
        p = jnp.broadcast_to(p[:, :, None, :], (2, half, g, HEAD_DIM))
        return p.reshape(2, 1, half * g * HEAD_DIM)

    b1 = jnp.concatenate([cmp_b1] * g, axis=-1)[:, None, :]
    w2 = jnp.concatenate([cmp_w2, cmp_w2], axis=-1).astype(BF16)
    return p_half(0), p_half(half), w_half(cmp_w1, 0), w_half(cmp_w1, half), b1, w2


def _overlap_t(seq):
    ncb = seq // CMP_STRIDE
    n_cmp = ncb - 1
    n_sel = seq // SEL_BLOCK
    c0 = np.arange(ncb)[None, :] * CMP_STRIDE
    s0 = np.arange(LANES)[:, None] * SEL_BLOCK
    ov = np.clip(np.minimum(c0 + CMP_BLOCK, s0 + SEL_BLOCK) - np.maximum(c0, s0), 0, None) / CMP_STRIDE
    ov = ov * (np.arange(ncb)[None, :] < n_cmp) * (np.arange(LANES)[:, None] < n_sel)
    return jnp.asarray(ov, F32)


def _gate_expand():
    r = np.arange(LANES)[None, :, None]
    col = np.arange(NSA_Q)[None, None, :]
    c = np.arange(3)[:, None, None]
    return jnp.asarray(r == (col // HEAD_DIM) * 3 + c, F32)


def _block_plan(counts, n_tok):
    rb = ROW_BLOCK
    nblk = -(-(n_tok * TOP_K + N_EXPERTS * (rb - 1)) // rb)
    padded = (counts + rb - 1) // rb * rb
    pad_end = jnp.cumsum(padded)
    pad_start = (pad_end - padded).astype(jnp.int32)
    first_row = jnp.arange(nblk, dtype=jnp.int32) * rb
    blk_expert = jnp.sum((pad_end[None, :] <= first_row[:, None]).astype(jnp.int32), axis=1)
    blk_expert = jnp.minimum(blk_expert, N_EXPERTS - 1).astype(jnp.int32)
    n_used = (pad_end[-1:] // rb).astype(jnp.int32)
    return pad_start, blk_expert, n_used, nblk


def kernel(x, w_in, b_in, cmp_pos, cmp_w1, cmp_b1, cmp_w2, w_proj_a, w_proj_b, w_out, ln1_g, ln1_b,
           router_w, router_bias, w_gate, w_up, w_down, ws_gate, ws_up, ws_down, ln2_g, ln2_b):
    bsz, seq, d = x.shape
    depth = w_in.shape[0]
    n = bsz * seq
    alpha = float((2 * depth) ** 0.25)
    cos, sa, sb = _rope_tables(seq)
    ovt = _overlap_t(seq)
    eg = _gate_expand()
    ncb = seq // CMP_STRIDE
    x2 = x.reshape(n, d)
    for l in range(depth):
        (qn, qnr, kc, vc, ks, kw, gates, qb, kb, ma, mb, vst, vwt, vbt) = _inproj(
            x2, *_pack_inproj(w_in[l], b_in[l]), cos, sa, sb, seq)
        cw = CMP_STRIDE * NSA_KV
        kcmp, vcmp = _compress(kc.reshape(bsz, ncb, cw), vc.reshape(bsz, ncb, cw),
                               *_pack_compress(cmp_pos[l], cmp_w1[l], cmp_b1[l], cmp_w2[l]))
        ocmp, member = _nsa_cmp(qn, kcmp, vcmp, ovt, bsz, seq)
        osel = _flash("sel", qnr, ks, vst, member, bsz, seq)
        owin = _flash("win", qnr, kw, vwt, member, bsz, seq)
        mmem = _moba_select(qb, _moba_kmean(kb, bsz, seq), bsz, seq)
        ob = _flash("moba", qb, kb, vbt, mmem, bsz, seq, tq=MOBA_TQ)
        x1, x1s = _post(ocmp, osel, owin, ob, gates, ma, mb, x2, eg,
                        w_proj_a[l].astype(BF16), w_proj_b[l].astype(BF16), w_out[l].astype(BF16),
                        ln1_g[l][None, :], ln1_b[l][None, :], alpha)
        top_idx, gate, rank, cnt = _router(x1, router_w[l].T.astype(BF16), router_bias[l][:, None])
        counts = cnt[:, 0].astype(jnp.int32)
        pad_start, blk_expert, n_used, nblk = _block_plan(counts, n)
        dest_tiles = _dest_rows(top_idx, rank, pad_start)
        xs = _dispatch(counts, pad_start, n_used, dest_tiles, x1s, nblk)
        y_slab = _experts(blk_expert, n_used, xs, w_gate, w_up, w_down, l)
        x2 = _combine(dest_tiles, y_slab, gate, x1, ws_gate[l].astype(BF16), ws_up[l].astype(BF16),
                      ws_down[l].astype(BF16), ln2_g[l][None, :], ln2_b[l][None, :], alpha)
    return x2.reshape(bsz, seq, d)
```

```python
import functools

import numpy as np
import jax
import jax.numpy as jnp
from jax import lax
from jax.experimental import pallas as pl
from jax.experimental.pallas import tpu as pltpu

D_MODEL = 1024
HEAD_DIM = 64
NSA_HEADS = 8
NSA_KV_HEADS = 2
MOBA_HEADS = 8
ROPE_THETA = 10000.0
CMP_BLOCK = 32
CMP_STRIDE = 16
CMP_HIDDEN = 128
SEL_BLOCK = 64
SEL_TOPN = 16
WINDOW = 512
MOBA_BLOCK = 256
MOBA_TOPK = 3
N_EXPERTS = 256
TOP_K = 8
N_GROUPS = 8
TOPK_GROUPS = 4
D_EXPERT = 256
ROUTED_SCALE = 2.5
LN_EPS = 1e-5
NEG = -1e30
BIG = 1e30
TAKEN = -3e38

NSA_Q = NSA_HEADS * HEAD_DIM
NSA_KV = NSA_KV_HEADS * HEAD_DIM
MOBA_W = MOBA_HEADS * HEAD_DIM
IN_SPLITS = (NSA_Q, NSA_KV, NSA_KV, NSA_KV, NSA_KV, NSA_KV, NSA_KV, 3 * NSA_HEADS,
             MOBA_W, MOBA_W, MOBA_W, D_MODEL, D_MODEL)

LANES = 128
SUBLANES = 8
SLAB = D_MODEL // (2 * LANES)
MOBA_SLOTS = LANES // MOBA_HEADS
ROW_BLOCK = 512
DISPATCH_TILE = 128
FLASH_TQ = 512
FLASH_TK = 512
MOBA_TQ = 1024
ONES_ROWS = 16
LOG2E = 1.4426950408889634
VMEM_LIMIT = 52 * 1024 * 1024

F32 = jnp.float32
BF16 = jnp.bfloat16
U32 = jnp.uint32
NT = (((1,), (1,)), ((), ()))


def _cp(sem, vmem=VMEM_LIMIT):
    return pltpu.CompilerParams(dimension_semantics=sem, vmem_limit_bytes=vmem)


def _iota(shape, axis):
    return lax.broadcasted_iota(jnp.int32, shape, axis)


def _inproj_layout():
    offs = np.concatenate([[0], np.cumsum(IN_SPLITS)])
    groups = []
    pieces = []

    def src(i):
        return np.arange(offs[i], offs[i + 1])

    def dup(i):
        c = src(i)
        return np.concatenate([c[:64], c[:64], c[64:], c[64:]])

    col = 0

    def add(cols, outs):
        nonlocal col
        pieces.append(cols)
        groups.append((col, len(cols), outs))
        col += len(cols)

    add(src(0), [(False, BF16), (True, BF16)])
    add(src(1), [(False, F32)])
    add(src(2), [(False, F32)])
    add(dup(3), [(True, BF16)])
    add(dup(5), [(True, BF16)])
    add(np.concatenate([src(7), -np.ones(LANES - 3 * NSA_HEADS, np.int64)]), [(False, F32)])
    add(src(8), [(True, BF16)])
    add(src(9), [(True, BF16)])
    add(src(11), [(False, F32)])
    add(src(12), [(False, F32)])
    tgroups, tpieces, row = [], [], 0
    for cols in (dup(4), dup(6), src(10)):
        tgroups.append((row, len(cols)))
        tpieces.append(cols)
        row += len(cols)
    return groups, np.concatenate(pieces), tgroups, np.concatenate(tpieces)


_INPROJ_GROUPS, _INPROJ_COLS, _INPROJ_TGROUPS, _INPROJ_TCOLS = _inproj_layout()


def _inproj_kernel(x_ref, w_ref, b_ref, wt_ref, bt_ref, cos_ref, sa_ref, sb_ref, *out_refs):
    xb = x_ref[...].astype(BF16)
    n_tok_major = sum(len(outs) for _, _, outs in _INPROJ_GROUPS)
    for (r0, rows), o_ref in zip(_INPROJ_TGROUPS, out_refs[n_tok_major:]):
        vt = lax.dot_general(wt_ref[r0:r0 + rows, :], xb, NT, preferred_element_type=F32) + bt_ref[r0:r0 + rows, :]
        o_ref[...] = vt.astype(BF16)
    oi = 0
    for c0, width, outs in _INPROJ_GROUPS:
        acc = jnp.dot(xb, w_ref[:, c0:c0 + width], preferred_element_type=F32) + b_ref[:, c0:c0 + width]
        for rope, dtype in outs:
            o_ref = out_refs[oi]
            oi += 1
            if not rope:
                o_ref[...] = acc.astype(dtype)
                continue
            cos = cos_ref[...]
            sa = sa_ref[...]
            sb = sb_ref[...]
            for j in range(0, width, LANES):
                t = acc[:, j:j + LANES]
                r = t * cos + pltpu.roll(t, LANES - HEAD_DIM // 2, 1) * sa + pltpu.roll(t, HEAD_DIM // 2, 1) * sb
                o_ref[:, j:j + LANES] = r.astype(dtype)


def _inproj(x2, w, b, wt, bt, cos, sa, sb, seq, tm=256):
    n = x2.shape[0]
    ctot = w.shape[1]
    rtot = wt.shape[0]
    nt_seq = seq // tm
    out_shape, out_specs = [], []
    for _, width, outs in _INPROJ_GROUPS:
        for _, dtype in outs:
            out_shape.append(jax.ShapeDtypeStruct((n, width), dtype))
            out_specs.append(pl.BlockSpec((tm, width), lambda i: (i, 0)))
    for _, rows in _INPROJ_TGROUPS:
        out_shape.append(jax.ShapeDtypeStruct((rows, n), BF16))
        out_specs.append(pl.BlockSpec((rows, tm), lambda i: (0, i)))
    tab = pl.BlockSpec((tm, LANES), lambda i: (i % nt_seq, 0))
    return pl.pallas_call(
        _inproj_kernel,
        grid=(n // tm,),
        in_specs=[pl.BlockSpec((tm, D_MODEL), lambda i: (i, 0)),
                  pl.BlockSpec((D_MODEL, ctot), lambda i: (0, 0)),
                  pl.BlockSpec((1, ctot), lambda i: (0, 0)),
                  pl.BlockSpec((rtot, D_MODEL), lambda i: (0, 0)),
                  pl.BlockSpec((rtot, 1), lambda i: (0, 0)),
                  tab, tab, tab],
        out_specs=out_specs,
        out_shape=out_shape,
        compiler_params=_cp(("parallel",)),
        name="inproj",
    )(x2, w, b, wt, bt, cos, sa, sb)


def _compress_kernel(kc_ref, vc_ref, ptop_ref, pbot_ref, wtop_ref, wbot_ref, b1_ref, w2_ref, ko_ref, vo_ref):
    ncb = kc_ref.shape[1]
    for kv, (c_ref, o_ref) in enumerate(((kc_ref, ko_ref), (vc_ref, vo_ref))):
        c = c_ref[0]
        a = jnp.dot((c + ptop_ref[kv]).astype(BF16), wtop_ref[kv], preferred_element_type=F32)
        bm = jnp.dot((c + pbot_ref[kv]).astype(BF16), wbot_ref[kv], preferred_element_type=F32)
        hid = jax.nn.gelu(a + pltpu.roll(bm, ncb - 1, 0) + b1_ref[kv])
        for g in range(NSA_KV_HEADS):
            hg = hid[:, g * CMP_HIDDEN:(g + 1) * CMP_HIDDEN].astype(BF16)
            o_ref[0, g] = jnp.dot(hg, w2_ref[kv], preferred_element_type=F32).astype(BF16)


def _compress(kc3, vc3, ptop, pbot, wtop, wbot, b1, w2):
    bsz, ncb, cw = kc3.shape
    hw = NSA_KV_HEADS * CMP_HIDDEN
    full = lambda shp: pl.BlockSpec(shp, lambda b: (0,) * len(shp))
    out = jax.ShapeDtypeStruct((bsz, NSA_KV_HEADS, ncb, LANES), BF16)
    ospec = pl.BlockSpec((1, NSA_KV_HEADS, ncb, LANES), lambda b: (b, 0, 0, 0))
    return pl.pallas_call(
        _compress_kernel,
        grid=(bsz,),
        in_specs=[pl.BlockSpec((1, ncb, cw), lambda b: (b, 0, 0)),
                  pl.BlockSpec((1, ncb, cw), lambda b: (b, 0, 0)),
                  full((2, 1, cw)), full((2, 1, cw)),
                  full((2, cw, hw)), full((2, cw, hw)),
                  full((2, 1, hw)), full((2, CMP_HIDDEN, LANES))],
        out_specs=[ospec, ospec],
        out_shape=[out, out],
        compiler_params=_cp(("parallel",)),
        name="nsa_compress",
    )(kc3, vc3, ptop, pbot, wtop, wbot, b1, w2)


def _rank_rows(val, n_rows):
    assert val.shape[0] % SUBLANES == 0
    width = val.shape[1]
    groups = [val[g:g + SUBLANES] for g in range(0, val.shape[0], SUBLANES)]
    ranks = [jnp.zeros((SUBLANES, width), F32) for _ in groups]
    sub = _iota((SUBLANES, width), 0)
    for i in range(n_rows):
        gi, ri = divmod(i, SUBLANES)
        vi = groups[gi][ri:ri + 1, :]
        for g, vg in enumerate(groups):
            if g < gi:
                ahead = vi > vg
            elif g > gi:
                ahead = vi >= vg
            else:
                ahead = (vi > vg) | ((vi == vg) & (sub > ri))
            ranks[g] = ranks[g] + jnp.where(ahead, 1.0, 0.0)
    return jnp.concatenate(ranks, axis=0)


def _transpose_to_rows(mt):
    tq = mt.shape[1]
    return jnp.concatenate([mt[:, c:c + LANES].T for c in range(0, tq, LANES)], axis=0)


def _nsa_cmp_kernel(q_ref, kc_ref, vc_ref, ov_ref, o_ref, mem_ref, *, tq, n_cmp, n_sel):
    i = pl.program_id(2)
    ncb = kc_ref.shape[2]
    kc = kc_ref[0, 0]
    vc = vc_ref[0, 0]
    pos = i * tq + _iota((tq, ncb), 0)
    blk = _iota((tq, ncb), 1)
    valid = (blk * CMP_STRIDE + CMP_BLOCK - 1 <= pos) & (blk < n_cmp)
    anyv = (i * tq + _iota((tq, 1), 0) >= CMP_BLOCK - 1).astype(F32)
    lane = _iota((1, LANES), 1)
    scale = HEAD_DIM ** -0.5
    psum = jnp.zeros((tq, ncb), F32)
    outs = []
    hg = NSA_HEADS // NSA_KV_HEADS
    for t in range(hg):
        qb = q_ref[:, (t // 2) * LANES:(t // 2 + 1) * LANES]
        half = (lane < HEAD_DIM) if t % 2 == 0 else (lane >= HEAD_DIM)
        qm = jnp.where(half, qb, jnp.zeros_like(qb))
        s = lax.dot_general(qm, kc, NT, preferred_element_type=F32) * scale
        s = jnp.where(valid, s, NEG)
        e = jnp.exp(s - jnp.max(s, axis=1, keepdims=True))
        p = e / jnp.sum(e, axis=1, keepdims=True) * anyv
        psum = psum + p
        outs.append(jnp.dot(p.astype(BF16), vc, preferred_element_type=F32))
    for r in range(hg // 2):
        o_ref[:, r * LANES:(r + 1) * LANES] = jnp.where(lane < HEAD_DIM, outs[2 * r], outs[2 * r + 1]).astype(BF16)

    imp = lax.dot_general(ov_ref[...], psum, NT, preferred_element_type=F32, precision=lax.Precision.HIGHEST)
    imp = imp[:n_sel]
    j = _iota((n_sel, tq), 0)
    cur = (i * tq + _iota((n_sel, tq), 1)) // SEL_BLOCK
    forced = (j == 0) | (j == cur) | (j == cur - 1)
    val = jnp.where(forced, BIG, jnp.where(j <= cur, imp, NEG))
    rank = _rank_rows(val, n_sel)
    member = jnp.where((rank < float(min(SEL_TOPN, n_sel))) & (j <= cur), 1.0, 0.0)
    if n_sel < LANES:
        member = jnp.concatenate([member, jnp.zeros((LANES - n_sel, tq), F32)], axis=0)
    mem_ref[0, 0] = member


def _nsa_cmp(qn, kcmp, vcmp, ovt, bsz, seq, tq=256):
    n = qn.shape[0]
    nq = seq // tq
    ncb = kcmp.shape[2]
    n_sel = seq // SEL_BLOCK
    gw = NSA_Q // NSA_KV_HEADS
    kern = functools.partial(_nsa_cmp_kernel, tq=tq, n_cmp=ncb - 1, n_sel=n_sel)
    kvspec = pl.BlockSpec((1, 1, ncb, LANES), lambda b, g, i: (b, g, 0, 0))
    return pl.pallas_call(
        kern,
        grid=(bsz, NSA_KV_HEADS, nq),
        in_specs=[pl.BlockSpec((tq, gw), lambda b, g, i: (b * nq + i, g)),
                  kvspec, kvspec,
                  pl.BlockSpec((LANES, ncb), lambda b, g, i: (0, 0))],
        out_specs=[pl.BlockSpec((tq, gw), lambda b, g, i: (b * nq + i, g)),
                   pl.BlockSpec((1, 1, LANES, tq), lambda b, g, i: (b, g, 0, i))],
        out_shape=[jax.ShapeDtypeStruct((n, NSA_Q), BF16),
                   jax.ShapeDtypeStruct((bsz, NSA_KV_HEADS, LANES, seq), F32)],
        compiler_params=_cp(("parallel", "parallel", "parallel")),
        name="nsa_cmp_select",
    )(qn, kcmp, vcmp, ovt)


def _moba_kmean_kernel(k_ref, o_ref):
    nblk = o_ref.shape[1]
    k = k_ref[...].astype(F32).reshape(nblk, MOBA_BLOCK, MOBA_W)
    o_ref[0] = jnp.sum(k, axis=1) * (1.0 / MOBA_BLOCK)


def _moba_kmean(kb, bsz, seq):
    nblk = seq // MOBA_BLOCK
    return pl.pallas_call(
        _moba_kmean_kernel,
        grid=(bsz,),
        in_specs=[pl.BlockSpec((seq, MOBA_W), lambda b: (b, 0))],
        out_specs=pl.BlockSpec((1, nblk, MOBA_W), lambda b: (b, 0, 0)),
        out_shape=jax.ShapeDtypeStruct((bsz, nblk, MOBA_W), F32),
        compiler_params=_cp(("parallel",)),
        name="moba_kmean",
    )(kb)


def _moba_select_kernel(q_ref, km_ref, mem_ref, *, tq, nblk):
    i = pl.program_id(1)
    km = km_ref[0]
    if nblk < MOBA_SLOTS:
        km = jnp.concatenate([km, jnp.zeros((MOBA_SLOTS - nblk, MOBA_W), F32)], axis=0)
    kmt = jnp.concatenate([km] * MOBA_HEADS, axis=0)
    r = _iota((LANES, MOBA_W), 0)
    c = _iota((LANES, MOBA_W), 1)
    kmt = jnp.where(r // MOBA_SLOTS == c // HEAD_DIM, kmt, 0.0).astype(BF16)
    gate = lax.dot_general(kmt, q_ref[...], NT, preferred_element_type=F32)
    members = []
    for h in range(MOBA_HEADS):
        sub = gate[h * MOBA_SLOTS:(h + 1) * MOBA_SLOTS]
        j = _iota((MOBA_SLOTS, tq), 0)
        cur = (i * tq + _iota((MOBA_SLOTS, tq), 1)) // MOBA_BLOCK
        past = j < cur
        val = jnp.where(past, sub, NEG)
        rank = _rank_rows(val, min(nblk, MOBA_SLOTS))
        members.append(jnp.where(((rank < float(MOBA_TOPK)) & past) | (j == cur), 1.0, 0.0))
    mem_ref[...] = jnp.concatenate(members, axis=0)


def _moba_select(qb, kmean, bsz, seq, tq=256):
    n = qb.shape[0]
    nq = seq // tq
    nblk = seq // MOBA_BLOCK
    kern = functools.partial(_moba_select_kernel, tq=tq, nblk=nblk)
    return pl.pallas_call(
        kern,
        grid=(bsz, nq),
        in_specs=[pl.BlockSpec((tq, MOBA_W), lambda b, i: (b * nq + i, 0)),
                  pl.BlockSpec((1, nblk, MOBA_W), lambda b, i: (b, 0, 0))],
        out_specs=pl.BlockSpec((LANES, tq), lambda b, i: (0, b * nq + i)),
        out_shape=jax.ShapeDtypeStruct((LANES, n), F32),
        compiler_params=_cp(("parallel", "parallel")),
        name="moba_select",
    )(qb, kmean)


def _flash_kernel(qi_ref, kj_ref, first_ref, last_ref, edge_ref, q_ref, k_ref, vt_ref, mem_ref, o_ref,
                  m_sc, acc_sc, qm_sc, *, mode, nhq, tq, tk):
    hg = pl.program_id(1)
    p = pl.program_id(2)
    i = qi_ref[p]
    jj = kj_ref[p]

    @pl.when(first_ref[p] == 1)
    def _():
        m_sc[...] = jnp.full(m_sc.shape, NEG, F32)
        acc_sc[...] = jnp.zeros(acc_sc.shape, F32)
        lane = _iota((1, LANES), 1)
        for t in range(nhq):
            qb = q_ref[:, (t // 2) * LANES:(t // 2 + 1) * LANES].astype(F32)
            half = (lane < HEAD_DIM) if t % 2 == 0 else (lane >= HEAD_DIM)
            qm_sc[t * tq:(t + 1) * tq, :] = jnp.where(half, qb * (HEAD_DIM ** -0.5 * LOG2E), 0.0).astype(BF16)

    def member_rows(first_row, keys_per_row):
        rows = [mem_ref[pl.ds(first_row + r, 1), :] > 0.5 for r in range(tk // keys_per_row)]
        return jnp.concatenate([jnp.broadcast_to(r, (keys_per_row, tq)) for r in rows], axis=0)

    def step(positional):
        allowed = None
        if positional:
            kpos = jj * tk + _iota((tk, tq), 0)
            qpos = i * tq + _iota((tk, tq), 1)
            allowed = kpos <= qpos
            if mode == "win":
                allowed = allowed & (kpos > qpos - WINDOW)
        if mode == "sel":
            memb = member_rows(jj * (tk // SEL_BLOCK), SEL_BLOCK)
            allowed = memb if allowed is None else allowed & memb
        k = k_ref[...]
        vt1 = jnp.concatenate([vt_ref[...], jnp.ones((ONES_ROWS, tk), BF16)], axis=0)
        s_all = lax.dot_general(k, qm_sc[...], NT, preferred_element_type=F32)
        prs, alphas = [], []
        for t in range(nhq):
            s = s_all[:, t * tq:(t + 1) * tq]
            ok = allowed
            if mode == "moba":
                memb = member_rows((hg * 2 + t) * MOBA_SLOTS + jj * (tk // MOBA_BLOCK), MOBA_BLOCK)
                ok = memb if ok is None else ok & memb
            if ok is not None:
                s = jnp.where(ok, s, NEG)
            m_old = m_sc[:, t * tq:(t + 1) * tq]
            m_new = jnp.maximum(m_old, jnp.max(s, axis=0, keepdims=True))
            prs.append(jnp.exp2(s - m_new).astype(BF16))
            alphas.append(jnp.exp2(m_old - m_new))
            m_sc[:, t * tq:(t + 1) * tq] = m_new
        pv = jnp.dot(vt1, jnp.concatenate(prs, axis=1), preferred_element_type=F32)
        acc_sc[...] = jnp.concatenate(alphas, axis=1) * acc_sc[...] + pv

    @pl.when(edge_ref[p] == 1)
    def _():
        step(True)

    @pl.when(edge_ref[p] == 0)
    def _():
        step(False)

    @pl.when(last_ref[p] == 1)
    def _():
        for r in range(nhq // 2):
            a0 = acc_sc[:, 2 * r * tq:(2 * r + 1) * tq]
            a1 = acc_sc[:, (2 * r + 1) * tq:(2 * r + 2) * tq]
            o0 = a0[:HEAD_DIM] / a0[LANES:LANES + 1]
            o1 = a1[HEAD_DIM:LANES] / a1[LANES:LANES + 1]
            ot = jnp.concatenate([o0, o1], axis=0)
            o_ref[:, r * LANES:(r + 1) * LANES] = _transpose_to_rows(ot).astype(BF16)


def _pair_tables(nq, tq, tk, window):
    qi, kj, first, last, edge = [], [], [], [], []
    for i in range(nq):
        q_lo, q_hi = i * tq, i * tq + tq - 1
        lo = 0 if window is None else max(0, (q_lo - window + 1) // tk)
        hi = q_hi // tk
        for j in range(lo, hi + 1):
            k_lo, k_hi = j * tk, j * tk + tk - 1
            inside = k_hi <= q_lo and (window is None or k_lo > q_hi - window)
            qi.append(i)
            kj.append(j)
            first.append(int(j == lo))
            last.append(int(j == hi))
            edge.append(int(not inside))
    mk = lambda a: jnp.asarray(np.asarray(a, np.int32))
    return mk(qi), mk(kj), mk(first), mk(last), mk(edge)


def _flash(mode, q, k, vt, mem, bsz, seq, tq=FLASH_TQ, tk=FLASH_TK):
    assert tk % MOBA_BLOCK == 0 and tk % SEL_BLOCK == 0 and seq % tq == 0 and seq % tk == 0
    n = q.shape[0]
    nq = seq // tq
    nk = seq // tk
    if mode == "moba":
        ngrp, nhq = MOBA_HEADS // 2, 2
        mem_spec = pl.BlockSpec((LANES, tq), lambda b, g, p, qi, kj, fi, la, ed: (0, b * nq + qi[p]))
    else:
        ngrp, nhq = NSA_KV_HEADS, NSA_HEADS // NSA_KV_HEADS
        mem_spec = pl.BlockSpec((None, None, LANES, tq), lambda b, g, p, qi, kj, fi, la, ed: (b, g, 0, qi[p]))
    tables = _pair_tables(nq, tq, tk, WINDOW if mode == "win" else None)
    npairs = int(tables[0].shape[0])
    qw = nhq // 2 * LANES
    kern = functools.partial(_flash_kernel, mode=mode, nhq=nhq, tq=tq, tk=tk)
    kspec = pl.BlockSpec((tk, LANES), lambda b, g, p, qi, kj, fi, la, ed: (b * nk + kj[p], g))
    vtspec = pl.BlockSpec((LANES, tk), lambda b, g, p, qi, kj, fi, la, ed: (g, b * nk + kj[p]))
    qspec = pl.BlockSpec((tq, qw), lambda b, g, p, qi, kj, fi, la, ed: (b * nq + qi[p], g))
    grid_spec = pltpu.PrefetchScalarGridSpec(
        num_scalar_prefetch=5,
        grid=(bsz, ngrp, npairs),
        in_specs=[qspec, kspec, vtspec, mem_spec],
        out_specs=qspec,
        scratch_shapes=[pltpu.VMEM((1, nhq * tq), F32),
                        pltpu.VMEM((LANES + ONES_ROWS, nhq * tq), F32),
                        pltpu.VMEM((nhq * tq, LANES), BF16)],
    )
    return pl.pallas_call(
        kern,
        grid_spec=grid_spec,
        out_shape=jax.ShapeDtypeStruct((n, ngrp * qw), BF16),
        compiler_params=_cp(("parallel", "parallel", "arbitrary")),
        name="flash_" + mode,
    )(*tables, q, k, vt, mem)


def _layer_norm(z, g, b):
    mu = jnp.mean(z, axis=-1, keepdims=True)
    zc = z - mu
    var = jnp.mean(zc * zc, axis=-1, keepdims=True)
    return zc * lax.rsqrt(var + LN_EPS) * g + b


HIGH_HALF = 0xFFFF0000


def _unpack_words(w):
    lo = lax.bitcast_convert_type(w << 16, F32)
    hi = lax.bitcast_convert_type(w & jnp.uint32(HIGH_HALF), F32)
    return lo, hi


def _store_slabs(slab_ref, val, rows):
    half = D_MODEL // 2
    lo = lax.bitcast_convert_type(val[:, :half].astype(BF16).astype(F32), U32) >> 16
    hi = lax.bitcast_convert_type(val[:, half:].astype(BF16).astype(F32), U32) & jnp.uint32(HIGH_HALF)
    w = lo | hi
    for c in range(SLAB):
        slab_ref[pl.ds(c, rows, stride=SLAB), :] = w[:, c * LANES:(c + 1) * LANES]


def _post_kernel(ocmp_ref, osel_ref, owin_ref, ob_ref, g_ref, ma_ref, mb_ref, x_ref,
                 eg_ref, wpa_ref, wpb_ref, wout_ref, lng_ref, lnb_ref, x1_ref, x1s_ref, *, alpha, tm):
    sig = jax.nn.sigmoid(g_ref[...])
    oa = jnp.zeros((tm, NSA_Q), F32)
    for c, o_ref in enumerate((ocmp_ref, osel_ref, owin_ref)):
        gexp = jnp.dot(sig, eg_ref[c], preferred_element_type=F32, precision=lax.Precision.HIGHEST)
        oa = oa + gexp * o_ref[...].astype(F32)
    pa = jnp.dot(oa.astype(BF16), wpa_ref[...], preferred_element_type=F32)
    pb = jnp.dot(ob_ref[...], wpb_ref[...], preferred_element_type=F32)
    merged = jax.nn.sigmoid(ma_ref[...]) * pa + jax.nn.sigmoid(mb_ref[...]) * pb
    y = jnp.dot(merged.astype(BF16), wout_ref[...], preferred_element_type=F32)
    x1 = _layer_norm(alpha * x_ref[...] + y, lng_ref[...], lnb_ref[...])
    x1_ref[...] = x1
    _store_slabs(x1s_ref, x1, tm)


def _post(ocmp, osel, owin, ob, gates, ma, mb, x2, eg, wpa, wpb, wout, lng, lnb, alpha, tm=256):
    n = x2.shape[0]
    row = lambda w: pl.BlockSpec((tm, w), lambda i: (i, 0))
    full = lambda shp: pl.BlockSpec(shp, lambda i: (0,) * len(shp))
    kern = functools.partial(_post_kernel, alpha=alpha, tm=tm)
    return pl.pallas_call(
        kern,
        grid=(n // tm,),
        in_specs=[row(NSA_Q), row(NSA_Q), row(NSA_Q), row(MOBA_W), row(LANES), row(D_MODEL), row(D_MODEL),
                  row(D_MODEL), full((3, LANES, NSA_Q)), full((NSA_Q, D_MODEL)), full((MOBA_W, D_MODEL)),
                  full((D_MODEL, D_MODEL)), full((1, D_MODEL)), full((1, D_MODEL))],
        out_specs=[row(D_MODEL), pl.BlockSpec((tm * SLAB, LANES), lambda i: (i, 0))],
        out_shape=[jax.ShapeDtypeStruct((n, D_MODEL), F32), jax.ShapeDtypeStruct((n * SLAB, LANES), U32)],
        compiler_params=_cp(("parallel",)),
        name="post_attention",
    )(ocmp, osel, owin, ob, gates, ma, mb, x2, eg, wpa, wpb, wout, lng, lnb)


def _first_argmax_rows(cur, rows, n_rows):
    mx = jnp.max(cur, axis=0, keepdims=True)
    idx = jnp.min(jnp.where(cur == mx, rows, n_rows), axis=0, keepdims=True)
    return mx, idx


def _router_kernel(x_ref, rwt_ref, bias_ref, idx_ref, gate_ref, rank_ref, cnt_ref, carry_sc, *, tm):
    i = pl.program_id(0)

    @pl.when(i == 0)
    def _():
        carry_sc[...] = jnp.zeros(carry_sc.shape, F32)

    logits = lax.dot_general(rwt_ref[...], x_ref[...].astype(BF16), NT, preferred_element_type=F32)
    sc = jax.nn.sigmoid(logits)
    biased = sc + bias_ref[...]
    gsz = N_EXPERTS // N_GROUPS
    grow = _iota((gsz, tm), 0)
    gscore = []
    for g in range(N_GROUPS):
        grp = biased[g * gsz:(g + 1) * gsz]
        m1, i1 = _first_argmax_rows(grp, grow, gsz)
        m2 = jnp.max(jnp.where(grow == i1, TAKEN, grp), axis=0, keepdims=True)
        gscore.append(m1 + m2)
    gs = jnp.concatenate(gscore, axis=0)
    keep = _rank_rows(gs, N_GROUPS) < float(TOPK_GROUPS)
    cur = jnp.concatenate(
        [jnp.where(keep[g:g + 1], biased[g * gsz:(g + 1) * gsz], NEG) for g in range(N_GROUPS)], axis=0)
    rows = _iota((N_EXPERTS, tm), 0)
    sels, tops, idxs = [], [], []
    for _ in range(TOP_K):
        _, ik = _first_argmax_rows(cur, rows, N_EXPERTS)
        selk = rows == ik
        tops.append(jnp.sum(jnp.where(selk, sc, 0.0), axis=0, keepdims=True))
        cur = jnp.where(selk, TAKEN, cur)
        sels.append(selk)
        idxs.append(ik)
    denom = tops[0]
    for tk_ in tops[1:]:
        denom = denom + tk_
    selall = jnp.where(cur == TAKEN, 1.0, 0.0)
    upper = jnp.where(_iota((tm, tm), 0) < _iota((tm, tm), 1), 1.0, 0.0).astype(BF16)
    before = jnp.dot(selall.astype(BF16), upper, preferred_element_type=F32) + carry_sc[...]
    idx_ref[...] = jnp.concatenate(idxs, axis=0)
    gate_ref[...] = jnp.concatenate([tk_ / denom * ROUTED_SCALE for tk_ in tops], axis=0)
    rank_ref[...] = jnp.concatenate(
        [jnp.sum(jnp.where(s, before, 0.0), axis=0, keepdims=True) for s in sels], axis=0).astype(jnp.int32)
    carry_sc[...] = carry_sc[...] + jnp.sum(selall, axis=1, keepdims=True)
    cnt_ref[...] = jnp.broadcast_to(carry_sc[...], cnt_ref.shape)


def _router(x1, rwt, bias, tm=256):
    n = x1.shape[0]
    kspec = pl.BlockSpec((TOP_K, tm), lambda i: (0, i))
    return pl.pallas_call(
        functools.partial(_router_kernel, tm=tm),
        grid=(n // tm,),
        in_specs=[pl.BlockSpec((tm, D_MODEL), lambda i: (i, 0)),
                  pl.BlockSpec((N_EXPERTS, D_MODEL), lambda i: (0, 0)),
                  pl.BlockSpec((N_EXPERTS, 1), lambda i: (0, 0))],
        out_specs=[kspec, kspec, kspec, pl.BlockSpec((N_EXPERTS, LANES), lambda i: (0, 0))],
        out_shape=[jax.ShapeDtypeStruct((TOP_K, n), jnp.int32), jax.ShapeDtypeStruct((TOP_K, n), F32),
                   jax.ShapeDtypeStruct((TOP_K, n), jnp.int32), jax.ShapeDtypeStruct((N_EXPERTS, LANES), F32)],
        scratch_shapes=[pltpu.VMEM((N_EXPERTS, 1), F32)],
        compiler_params=_cp(("arbitrary",)),
        name="router",
    )(x1, rwt, bias)


def _dest_kernel(idx_ref, rank_ref, pstart_ref, o_ref, *, tm):
    rows = _iota((N_EXPERTS, tm), 0)
    pstart = pstart_ref[...]
    for k in range(TOP_K):
        hit = rows == idx_ref[k:k + 1, :]
        base = jnp.sum(jnp.where(hit, pstart, 0.0), axis=0, keepdims=True)
        o_ref[:, k * tm:(k + 1) * tm] = base.astype(jnp.int32) + rank_ref[k:k + 1, :]


def _dest_rows(top_idx, rank, pad_start, tm=DISPATCH_TILE):
    n = top_idx.shape[1]
    kspec = pl.BlockSpec((TOP_K, tm), lambda i: (0, i))
    out = pl.pallas_call(
        functools.partial(_dest_kernel, tm=tm),
        grid=(n // tm,),
        in_specs=[kspec, kspec, pl.BlockSpec((N_EXPERTS, 1), lambda i: (0, 0))],
        out_specs=pl.BlockSpec((None, 1, TOP_K * tm), lambda i: (i, 0, 0)),
        out_shape=jax.ShapeDtypeStruct((n // tm, 1, TOP_K * tm), jnp.int32),
        compiler_params=_cp(("parallel",)),
        name="moe_dest_rows",
    )(top_idx, rank, pad_start.astype(F32)[:, None])
    return out.reshape(n // tm, TOP_K, tm).transpose(0, 2, 1).reshape(n // tm, tm * TOP_K)


def _issue_per_token(tm, issue):
    def body(t, carry):
        for k in range(TOP_K):
            issue(t, k)
        return carry
    lax.fori_loop(0, tm, body, 0)


def _dispatch_kernel(cnt_ref, pstart_ref, nused_ref, dest_hbm, x_ref, xs_hbm, idx_sm, zero_sc, idx_sem, row_sem,
                     zero_sem, *, ntile, tm, nblk):
    i = pl.program_id(0)
    slot = i % 2
    nrow = TOP_K * tm

    def idx_copy(t, s):
        return pltpu.make_async_copy(dest_hbm.at[t], idx_sm.at[s], idx_sem.at[s])

    def zero_copy(row, rows):
        return pltpu.make_async_copy(zero_sc.at[pl.ds(0, rows * SLAB)],
                                     xs_hbm.at[pl.ds(pl.multiple_of(row * SLAB, SLAB), rows * SLAB)], zero_sem.at[0])

    def zero_fill(act):
        def per_expert(e, carry):
            cnt = cnt_ref[e]
            npad = (ROW_BLOCK - cnt % ROW_BLOCK) % ROW_BLOCK
            row = pstart_ref[e] + cnt
            piece = ROW_BLOCK // 2
            while piece >= 1:
                @pl.when((npad & piece) != 0)
                def _(row=row, piece=piece):
                    act(zero_copy(row, piece))
                row = row + (npad & piece)
                piece //= 2
            return carry
        lax.fori_loop(0, N_EXPERTS, per_expert, 0)

        def per_block(bk, carry):
            act(zero_copy(bk * ROW_BLOCK, ROW_BLOCK))
            return carry
        lax.fori_loop(nused_ref[0], nblk, per_block, 0)

    @pl.when(i == 0)
    def _():
        idx_copy(0, 0).start()
        if ntile > 1:
            idx_copy(1, 1).start()
        zero_sc[...] = jnp.zeros(zero_sc.shape, U32)
        zero_fill(lambda cp: cp.start())
        zero_fill(lambda cp: cp.wait())

    idx_copy(i, slot).wait()

    def issue(t, k):
        src = x_ref.at[pl.ds(pl.multiple_of(t * SLAB, SLAB), SLAB)]
        dst = xs_hbm.at[pl.ds(pl.multiple_of(idx_sm[slot, t * TOP_K + k] * SLAB, SLAB), SLAB)]
        pltpu.make_async_copy(src, dst, row_sem.at[0]).start()
    _issue_per_token(tm, issue)

    @pl.when(i + 2 < ntile)
    def _():
        idx_copy(i + 2, slot).start()

    for _ in range(TOP_K):
        pltpu.make_async_copy(x_ref, xs_hbm.at[pl.ds(0, tm * SLAB)], row_sem.at[0]).wait()


def _dispatch(counts, pad_start, n_used, dest_tiles, x1s, nblk, tm=DISPATCH_TILE):
    ntile = dest_tiles.shape[0]
    grid_spec = pltpu.PrefetchScalarGridSpec(
        num_scalar_prefetch=3,
        grid=(ntile,),
        in_specs=[pl.BlockSpec(memory_space=pl.ANY),
                  pl.BlockSpec((tm * SLAB, LANES), lambda i, cn, ps, nu: (i, 0))],
        out_specs=pl.BlockSpec(memory_space=pl.ANY),
        scratch_shapes=[pltpu.SMEM((2, TOP_K * tm), jnp.int32),
                        pltpu.VMEM((ROW_BLOCK * SLAB, LANES), U32),
                        pltpu.SemaphoreType.DMA((2,)),
                        pltpu.SemaphoreType.DMA((1,)),
                        pltpu.SemaphoreType.DMA((1,))],
    )
    return pl.pallas_call(
        functools.partial(_dispatch_kernel, ntile=ntile, tm=tm, nblk=nblk),
        grid_spec=grid_spec,
        out_shape=jax.ShapeDtypeStruct((nblk * ROW_BLOCK * SLAB, LANES), U32),
        compiler_params=_cp(("arbitrary",)),
        name="moe_dispatch",
    )(counts, pad_start, n_used, dest_tiles, x1s)


def _expert_kernel(be_ref, nused_ref, xs_ref, wg_ref, wu_ref, wd_ref, y_ref, xmat, wgu_sc, wd_sc):
    i = pl.program_id(0)
    rb = ROW_BLOCK

    @pl.when(i < nused_ref[0])
    def _():
        e = be_ref[i]
        prev = be_ref[jnp.maximum(i - 1, 0)]

        @pl.when((i == 0) | (e != prev))
        def _():
            wgu_sc[:, :D_EXPERT] = wg_ref[0, 0].astype(BF16)
            wgu_sc[:, D_EXPERT:] = wu_ref[0, 0].astype(BF16)
            wd_sc[...] = wd_ref[0, 0].astype(BF16)

        half = D_MODEL // 2
        for c in range(SLAB):
            lo, hi = _unpack_words(xs_ref[pl.ds(c, rb, stride=SLAB), :])
            xmat[:, c * LANES:(c + 1) * LANES] = lo.astype(BF16)
            xmat[:, half + c * LANES:half + (c + 1) * LANES] = hi.astype(BF16)
        hgu = jnp.dot(xmat[...], wgu_sc[...], preferred_element_type=F32)
        h = jax.nn.silu(hgu[:, :D_EXPERT]) * hgu[:, D_EXPERT:]
        y = jnp.dot(h.astype(BF16), wd_sc[...], preferred_element_type=F32)
        _store_slabs(y_ref, y, rb)

    @pl.when(i >= nused_ref[0])
    def _():
        y_ref[...] = jnp.zeros(y_ref.shape, U32)


def _experts(blk_expert, n_used, xs, w_gate, w_up, w_down, layer):
    nblk = blk_expert.shape[0]
    rb = ROW_BLOCK
    blk = lambda i, nu: jnp.minimum(i, nu[0] - 1)
    wspec = lambda shp: pl.BlockSpec((1, 1) + shp, lambda i, be, nu: (layer, be[blk(i, nu)], 0, 0))
    grid_spec = pltpu.PrefetchScalarGridSpec(
        num_scalar_prefetch=2,
        grid=(nblk,),
        in_specs=[pl.BlockSpec((rb * SLAB, LANES), lambda i, be, nu: (blk(i, nu), 0)),
                  wspec((D_MODEL, D_EXPERT)), wspec((D_MODEL, D_EXPERT)), wspec((D_EXPERT, D_MODEL))],
        out_specs=pl.BlockSpec((rb * SLAB, LANES), lambda i, be, nu: (i, 0)),
        scratch_shapes=[pltpu.VMEM((rb, D_MODEL), BF16),
                        pltpu.VMEM((D_MODEL, 2 * D_EXPERT), BF16),
                        pltpu.VMEM((D_EXPERT, D_MODEL), BF16)],
    )
    return pl.pallas_call(
        _expert_kernel,
        grid_spec=grid_spec,
        out_shape=jax.ShapeDtypeStruct((nblk * rb * SLAB, LANES), U32),
        compiler_params=_cp(("arbitrary",)),
        name="moe_experts",
    )(blk_expert, n_used, xs, w_gate, w_up, w_down)


def _combine_kernel(dest_hbm, y_hbm, gate_ref, x1_ref, wsg_ref, wsu_ref, wsd_ref, lng_ref, lnb_ref, o_ref,
                    idx_sm, ybuf, gbuf, acc_sc, idx_sem, row_sem, *, ntile, tm, alpha):
    i = pl.program_id(0)
    slot = i % 2
    nrow = TOP_K * tm

    def idx_copy(t, s):
        return pltpu.make_async_copy(dest_hbm.at[t], idx_sm.at[s], idx_sem.at[s])

    def gather(s):
        def issue(t, k):
            src = y_hbm.at[pl.ds(pl.multiple_of(idx_sm[s, t * TOP_K + k] * SLAB, SLAB), SLAB)]
            dst = ybuf.at[s, pl.ds(pl.multiple_of((k * tm + t) * SLAB, SLAB), SLAB)]
            pltpu.make_async_copy(src, dst, row_sem.at[s]).start()
        _issue_per_token(tm, issue)

    @pl.when(i == 0)
    def _():
        idx_copy(0, 0).start()
        idx_copy(0, 0).wait()
        gather(0)
        if ntile > 1:
            idx_copy(1, 1).start()

    @pl.when(i + 1 < ntile)
    def _():
        idx_copy(i + 1, 1 - slot).wait()
        gather(1 - slot)

    @pl.when(i + 2 < ntile)
    def _():
        idx_copy(i + 2, slot).start()

    x1 = x1_ref[...]
    xb = x1.astype(BF16)
    hs = jax.nn.silu(jnp.dot(xb, wsg_ref[...], preferred_element_type=F32)) * jnp.dot(
        xb, wsu_ref[...], preferred_element_type=F32)
    shared = jnp.dot(hs.astype(BF16), wsd_ref[...], preferred_element_type=F32)

    for k in range(TOP_K):
        gcol = _transpose_to_rows(jnp.broadcast_to(gate_ref[k:k + 1, :], (LANES, tm)))
        for c in range(SLAB):
            gbuf[pl.ds(k * tm * SLAB + c, tm, stride=SLAB), :] = gcol

    pltpu.make_async_copy(y_hbm.at[pl.ds(0, nrow * SLAB)], ybuf.at[slot], row_sem.at[slot]).wait()
    acc_lo = jnp.zeros((tm * SLAB, LANES), F32)
    acc_hi = jnp.zeros((tm * SLAB, LANES), F32)
    for k in range(TOP_K):
        lo, hi = _unpack_words(ybuf[slot, pl.ds(k * tm * SLAB, tm * SLAB), :])
        g = gbuf[pl.ds(k * tm * SLAB, tm * SLAB), :]
        acc_lo = acc_lo + lo * g
        acc_hi = acc_hi + hi * g
    acc_sc[0] = acc_lo
    acc_sc[1] = acc_hi
    routed = jnp.concatenate([acc_sc[h, pl.ds(c, tm, stride=SLAB), :] for h in range(2) for c in range(SLAB)],
                             axis=1)
    o_ref[...] = _layer_norm(alpha * x1 + routed + shared, lng_ref[...], lnb_ref[...])


def _combine(dest_tiles, y_slab, gate, x1, wsg, wsu, wsd, lng, lnb, alpha, tm=DISPATCH_TILE):
    n = x1.shape[0]
    ntile = n // tm
    nrow = TOP_K * tm
    full = lambda shp: pl.BlockSpec(shp, lambda i: (0,) * len(shp))
    kern = functools.partial(_combine_kernel, ntile=ntile, tm=tm, alpha=alpha)
    return pl.pallas_call(
        kern,
        grid=(ntile,),
        in_specs=[pl.BlockSpec(memory_space=pl.ANY),
                  pl.BlockSpec(memory_space=pl.ANY),
                  pl.BlockSpec((TOP_K, tm), lambda i: (0, i)),
                  pl.BlockSpec((tm, D_MODEL), lambda i: (i, 0)),
                  full((D_MODEL, D_EXPERT)), full((D_MODEL, D_EXPERT)), full((D_EXPERT, D_MODEL)),
                  full((1, D_MODEL)), full((1, D_MODEL))],
        out_specs=pl.BlockSpec((tm, D_MODEL), lambda i: (i, 0)),
        out_shape=jax.ShapeDtypeStruct((n, D_MODEL), F32),
        scratch_shapes=[pltpu.SMEM((2, nrow), jnp.int32),
                        pltpu.VMEM((2, nrow * SLAB, LANES), U32),
                        pltpu.VMEM((nrow * SLAB, LANES), F32),
                        pltpu.VMEM((2, tm * SLAB, LANES), F32),
                        pltpu.SemaphoreType.DMA((2,)),
                        pltpu.SemaphoreType.DMA((2,))],
        compiler_params=_cp(("arbitrary",)),
        name="moe_combine",
    )(dest_tiles, y_slab, gate, x1, wsg, wsu, wsd, lng, lnb)


def _rope_tables(seq):
    pos = jnp.arange(seq, dtype=F32)
    inv = ROPE_THETA ** (-jnp.arange(0, HEAD_DIM, 2, dtype=F32) / HEAD_DIM)
    ang = pos[:, None] * inv[None, :]
    ang = jnp.concatenate([ang] * (LANES // (HEAD_DIM // 2)), axis=-1)
    first = (np.arange(LANES) % HEAD_DIM) < HEAD_DIM // 2
    sin = jnp.sin(ang)
    return jnp.cos(ang), jnp.where(first, -sin, 0.0), jnp.where(first, 0.0, sin)


def _pack_inproj(w_in, b_in):
    cols = _INPROJ_COLS
    safe = np.where(cols < 0, 0, cols)
    live = jnp.asarray(cols >= 0)
    w = jnp.where(live[None, :], w_in[:, safe], 0.0).astype(BF16)
    b = jnp.where(live, b_in[safe], 0.0)[None, :]
    wt = w_in[:, _INPROJ_TCOLS].T.astype(BF16)
    bt = b_in[_INPROJ_TCOLS][:, None]
    return w, b, wt, bt


def _pack_compress(cmp_pos, cmp_w1, cmp_b1, cmp_w2):
    half = CMP_BLOCK // 2
    g = NSA_KV_HEADS

    def w_half(w1, lo):
        w = w1.reshape(2, CMP_BLOCK, HEAD_DIM, CMP_HIDDEN)[:, lo:lo + half]
        eye = jnp.eye(g, dtype=F32)
        w = jnp.einsum("klde,gh->klgdhe", w, eye)
        return w.reshape(2, half * g * HEAD_DIM, g * CMP_HIDDEN).astype(BF16)

    def p_half(lo):
        p = cmp_pos[:, lo:lo + half]
        p = jnp.broadcast_to(p[:, :, None, :], (2, half, g, HEAD_DIM))
        return p.reshape(2, 1, half * g * HEAD_DIM)

    b1 = jnp.concatenate([cmp_b1] * g, axis=-1)[:, None, :]
    w2 = jnp.concatenate([cmp_w2, cmp_w2], axis=-1).astype(BF16)
    return p_half(0), p_half(half), w_half(cmp_w1, 0), w_half(cmp_w1, half), b1, w2


def _overlap_t(seq):
    ncb = seq // CMP_STRIDE
    n_cmp = ncb - 1
    n_sel = seq // SEL_BLOCK
    c0 = np.arange(ncb)[None, :] * CMP_STRIDE
    s0 = np.arange(LANES)[:, None] * SEL_BLOCK
    ov = np.clip(np.minimum(c0 + CMP_BLOCK, s0 + SEL_BLOCK) - np.maximum(c0, s0), 0, None) / CMP_STRIDE
    ov = ov * (np.arange(ncb)[None, :] < n_cmp) * (np.arange(LANES)[:, None] < n_sel)
    return jnp.asarray(ov, F32)


def _gate_expand():
    r = np.arange(LANES)[None, :, None]
    col = np.arange(NSA_Q)[None, None, :]
    c = np.arange(3)[:, None, None]
    return jnp.asarray(r == (col // HEAD_DIM) * 3 + c, F32)


def _block_plan(counts, n_tok):
    rb = ROW_BLOCK
    nblk = -(-(n_tok * TOP_K + N_EXPERTS * (rb - 1)) // rb)
    padded = (counts + rb - 1) // rb * rb
    pad_end = jnp.cumsum(padded)
    pad_start = (pad_end - padded).astype(jnp.int32)
    first_row = jnp.arange(nblk, dtype=jnp.int32) * rb
    blk_expert = jnp.sum((pad_end[None, :] <= first_row[:, None]).astype(jnp.int32), axis=1)
    blk_expert = jnp.minimum(blk_expert, N_EXPERTS - 1).astype(jnp.int32)
    n_used = (pad_end[-1:] // rb).astype(jnp.int32)
    return pad_start, blk_expert, n_used, nblk


def kernel(x, w_in, b_in, cmp_pos, cmp_w1, cmp_b1, cmp_w2, w_proj_a, w_proj_b, w_out, ln1_g, ln1_b,
           router_w, router_bias, w_gate, w_up, w_down, ws_gate, ws_up, ws_down, ln2_g, ln2_b):
    bsz, seq, d = x.shape
    depth = w_in.shape[0]
    n = bsz * seq
    alpha = float((2 * depth) ** 0.25)
    cos, sa, sb = _rope_tables(seq)
    ovt = _overlap_t(seq)
    eg = _gate_expand()
    ncb = seq // CMP_STRIDE
    x2 = x.reshape(n, d)
    for l in range(depth):
        (qn, qnr, kc, vc, ks, kw, gates, qb, kb, ma, mb, vst, vwt, vbt) = _inproj(
            x2, *_pack_inproj(w_in[l], b_in[l]), cos, sa, sb, seq)
        cw = CMP_STRIDE * NSA_KV
        kcmp, vcmp = _compress(kc.reshape(bsz, ncb, cw), vc.reshape(bsz, ncb, cw),
                               *_pack_compress(cmp_pos[l], cmp_w1[l], cmp_b1[l], cmp_w2[l]))
        ocmp, member = _nsa_cmp(qn, kcmp, vcmp, ovt, bsz, seq)
        osel = _flash("sel", qnr, ks, vst, member, bsz, seq)
        owin = _flash("win", qnr, kw, vwt, member, bsz, seq)
        mmem = _moba_select(qb, _moba_kmean(kb, bsz, seq), bsz, seq)
        ob = _flash("moba", qb, kb, vbt, mmem, bsz, seq, tq=MOBA_TQ)
        x1, x1s = _post(ocmp, osel, owin, ob, gates, ma, mb, x2, eg,
                        w_proj_a[l].astype(BF16), w_proj_b[l].astype(BF16), w_out[l].astype(BF16),
                        ln1_g[l][None, :], ln1_b[l][None, :], alpha)
        top_idx, gate, rank, cnt = _router(x1, router_w[l].T.astype(BF16), router_bias[l][:, None])
        counts = cnt[:, 0].astype(jnp.int32)
        pad_start, blk_expert, n_used, nblk = _block_plan(counts, n)
        dest_tiles = _dest_rows(top_idx, rank, pad_start)
        xs = _dispatch(counts, pad_start, n_used, dest_tiles, x1s, nblk)
        y_slab = _experts(blk_expert, n_used, xs, w_gate, w_up, w_down, l)
        x2 = _combine(dest_tiles, y_slab, gate, x1, ws_gate[l].astype(BF16), ws_up[l].astype(BF16),
                      ws_down[l].astype(BF16), ln2_g[l][None, :], ln2_b[l][None, :], alpha)
    return x2.reshape(bsz, seq, d)
```

```python
import functools

import numpy as np
import jax
import jax.numpy as jnp
from jax import lax
from jax.experimental import pallas as pl
from jax.experimental.pallas import tpu as pltpu

D_MODEL = 1024
HEAD_DIM = 64
NSA_HEADS = 8
NSA_KV_HEADS = 2
MOBA_HEADS = 8
ROPE_THETA = 10000.0
CMP_BLOCK = 32
CMP_STRIDE = 16
CMP_HIDDEN = 128
SEL_BLOCK = 64
SEL_TOPN = 16
WINDOW = 512
MOBA_BLOCK = 256
MOBA_TOPK = 3
N_EXPERTS = 256
TOP_K = 8
N_GROUPS = 8
TOPK_GROUPS = 4
D_EXPERT = 256
ROUTED_SCALE = 2.5
LN_EPS = 1e-5
NEG = -1e30
BIG = 1e30
TAKEN = -3e38

NSA_Q = NSA_HEADS * HEAD_DIM
NSA_KV = NSA_KV_HEADS * HEAD_DIM
MOBA_W = MOBA_HEADS * HEAD_DIM
IN_SPLITS = (NSA_Q, NSA_KV, NSA_KV, NSA_KV, NSA_KV, NSA_KV, NSA_KV, 3 * NSA_HEADS,
             MOBA_W, MOBA_W, MOBA_W, D_MODEL, D_MODEL)

LANES = 128
SUBLANES = 8
SLAB = D_MODEL // (2 * LANES)
MOBA_SLOTS = LANES // MOBA_HEADS
ROW_BLOCK = 512
DISPATCH_TILE = 128
FLASH_TQ = 512
FLASH_TK = 512
MOBA_TQ = 1024
ONES_ROWS = 16
LOG2E = 1.4426950408889634
VMEM_LIMIT = 52 * 1024 * 1024

F32 = jnp.float32
BF16 = jnp.bfloat16
U32 = jnp.uint32
NT = (((1,), (1,)), ((), ()))


def _cp(sem, vmem=VMEM_LIMIT):
    return pltpu.CompilerParams(dimension_semantics=sem, vmem_limit_bytes=vmem)


def _iota(shape, axis):
    return lax.broadcasted_iota(jnp.int32, shape, axis)


def _inproj_layout():
    offs = np.concatenate([[0], np.cumsum(IN_SPLITS)])
    groups = []
    pieces = []

    def src(i):
        return np.arange(offs[i], offs[i + 1])

    def dup(i):
        c = src(i)
        return np.concatenate([c[:64], c[:64], c[64:], c[64:]])

    col = 0

    def add(cols, outs):
        nonlocal col
        pieces.append(cols)
        groups.append((col, len(cols), outs))
        col += len(cols)

    add(src(0), [(False, BF16), (True, BF16)])
    add(src(1), [(False, F32)])
    add(src(2), [(False, F32)])
    add(dup(3), [(True, BF16)])
    add(dup(5), [(True, BF16)])
    add(np.concatenate([src(7), -np.ones(LANES - 3 * NSA_HEADS, np.int64)]), [(False, F32)])
    add(src(8), [(True, BF16)])
    add(src(9), [(True, BF16)])
    add(src(11), [(False, F32)])
    add(src(12), [(False, F32)])
    tgroups, tpieces, row = [], [], 0
    for cols in (dup(4), dup(6), src(10)):
        tgroups.append((row, len(cols)))
        tpieces.append(cols)
        row += len(cols)
    return groups, np.concatenate(pieces), tgroups, np.concatenate(tpieces)


_INPROJ_GROUPS, _INPROJ_COLS, _INPROJ_TGROUPS, _INPROJ_TCOLS = _inproj_layout()


def _inproj_kernel(x_ref, w_ref, b_ref, wt_ref, bt_ref, cos_ref, sa_ref, sb_ref, *out_refs):
    xb = x_ref[...].astype(BF16)
    n_tok_major = sum(len(outs) for _, _, outs in _INPROJ_GROUPS)
    for (r0, rows), o_ref in zip(_INPROJ_TGROUPS, out_refs[n_tok_major:]):
        vt = lax.dot_general(wt_ref[r0:r0 + rows, :], xb, NT, preferred_element_type=F32) + bt_ref[r0:r0 + rows, :]
        o_ref[...] = vt.astype(BF16)
    oi = 0
    for c0, width, outs in _INPROJ_GROUPS:
        acc = jnp.dot(xb, w_ref[:, c0:c0 + width], preferred_element_type=F32) + b_ref[:, c0:c0 + width]
        for rope, dtype in outs:
            o_ref = out_refs[oi]
            oi += 1
            if not rope:
                o_ref[...] = acc.astype(dtype)
                continue
            cos = cos_ref[...]
            sa = sa_ref[...]
            sb = sb_ref[...]
            for j in range(0, width, LANES):
                t = acc[:, j:j + LANES]
                r = t * cos + pltpu.roll(t, LANES - HEAD_DIM // 2, 1) * sa + pltpu.roll(t, HEAD_DIM // 2, 1) * sb
                o_ref[:, j:j + LANES] = r.astype(dtype)


def _inproj(x2, w, b, wt, bt, cos, sa, sb, seq, tm=256):
    n = x2.shape[0]
    ctot = w.shape[1]
    rtot = wt.shape[0]
    nt_seq = seq // tm
    out_shape, out_specs = [], []
    for _, width, outs in _INPROJ_GROUPS:
        for _, dtype in outs:
            out_shape.append(jax.ShapeDtypeStruct((n, width), dtype))
            out_specs.append(pl.BlockSpec((tm, width), lambda i: (i, 0)))
    for _, rows in _INPROJ_TGROUPS:
        out_shape.append(jax.ShapeDtypeStruct((rows, n), BF16))
        out_specs.append(pl.BlockSpec((rows, tm), lambda i: (0, i)))
    tab = pl.BlockSpec((tm, LANES), lambda i: (i % nt_seq, 0))
    return pl.pallas_call(
        _inproj_kernel,
        grid=(n // tm,),
        in_specs=[pl.BlockSpec((tm, D_MODEL), lambda i: (i, 0)),
                  pl.BlockSpec((D_MODEL, ctot), lambda i: (0, 0)),
                  pl.BlockSpec((1, ctot), lambda i: (0, 0)),
                  pl.BlockSpec((rtot, D_MODEL), lambda i: (0, 0)),
                  pl.BlockSpec((rtot, 1), lambda i: (0, 0)),
                  tab, tab, tab],
        out_specs=out_specs,
        out_shape=out_shape,
        compiler_params=_cp(("parallel",)),
        name="inproj",
    )(x2, w, b, wt, bt, cos, sa, sb)


def _compress_kernel(kc_ref, vc_ref, ptop_ref, pbot_ref, wtop_ref, wbot_ref, b1_ref, w2_ref, ko_ref, vo_ref):
    ncb = kc_ref.shape[1]
    for kv, (c_ref, o_ref) in enumerate(((kc_ref, ko_ref), (vc_ref, vo_ref))):
        c = c_ref[0]
        a = jnp.dot((c + ptop_ref[kv]).astype(BF16), wtop_ref[kv], preferred_element_type=F32)
        bm = jnp.dot((c + pbot_ref[kv]).astype(BF16), wbot_ref[kv], preferred_element_type=F32)
        hid = jax.nn.gelu(a + pltpu.roll(bm, ncb - 1, 0) + b1_ref[kv])
        for g in range(NSA_KV_HEADS):
            hg = hid[:, g * CMP_HIDDEN:(g + 1) * CMP_HIDDEN].astype(BF16)
            o_ref[0, g] = jnp.dot(hg, w2_ref[kv], preferred_element_type=F32).astype(BF16)


def _compress(kc3, vc3, ptop, pbot, wtop, wbot, b1, w2):
    bsz, ncb, cw = kc3.shape
    hw = NSA_KV_HEADS * CMP_HIDDEN
    full = lambda shp: pl.BlockSpec(shp, lambda b: (0,) * len(shp))
    out = jax.ShapeDtypeStruct((bsz, NSA_KV_HEADS, ncb, LANES), BF16)
    ospec = pl.BlockSpec((1, NSA_KV_HEADS, ncb, LANES), lambda b: (b, 0, 0, 0))
    return pl.pallas_call(
        _compress_kernel,
        grid=(bsz,),
        in_specs=[pl.BlockSpec((1, ncb, cw), lambda b: (b, 0, 0)),
                  pl.BlockSpec((1, ncb, cw), lambda b: (b, 0, 0)),
                  full((2, 1, cw)), full((2, 1, cw)),
                  full((2, cw, hw)), full((2, cw, hw)),
                  full((2, 1, hw)), full((2, CMP_HIDDEN, LANES))],
        out_specs=[ospec, ospec],
        out_shape=[out, out],
        compiler_params=_cp(("parallel",)),
        name="nsa_compress",
    )(kc3, vc3, ptop, pbot, wtop, wbot, b1, w2)


def _rank_rows(val, n_rows):
    assert val.shape[0] % SUBLANES == 0
    width = val.shape[1]
    groups = [val[g:g + SUBLANES] for g in range(0, val.shape[0], SUBLANES)]
    ranks = [jnp.zeros((SUBLANES, width), F32) for _ in groups]
    sub = _iota((SUBLANES, width), 0)
    for i in range(n_rows):
        gi, ri = divmod(i, SUBLANES)
        vi = groups[gi][ri:ri + 1, :]
        for g, vg in enumerate(groups):
            if g < gi:
                ahead = vi > vg
            elif g > gi:
                ahead = vi >= vg
            else:
                ahead = (vi > vg) | ((vi == vg) & (sub > ri))
            ranks[g] = ranks[g] + jnp.where(ahead, 1.0, 0.0)
    return jnp.concatenate(ranks, axis=0)


def _transpose_to_rows(mt):
    tq = mt.shape[1]
    return jnp.concatenate([mt[:, c:c + LANES].T for c in range(0, tq, LANES)], axis=0)


def _nsa_cmp_kernel(q_ref, kc_ref, vc_ref, ov_ref, o_ref, mem_ref, *, tq, n_cmp, n_sel):
    i = pl.program_id(2)
    ncb = kc_ref.shape[2]
    kc = kc_ref[0, 0]
    vc = vc_ref[0, 0]
    pos = i * tq + _iota((tq, ncb), 0)
    blk = _iota((tq, ncb), 1)
    valid = (blk * CMP_STRIDE + CMP_BLOCK - 1 <= pos) & (blk < n_cmp)
    anyv = (i * tq + _iota((tq, 1), 0) >= CMP_BLOCK - 1).astype(F32)
    lane = _iota((1, LANES), 1)
    scale = HEAD_DIM ** -0.5
    psum = jnp.zeros((tq, ncb), F32)
    outs = []
    hg = NSA_HEADS // NSA_KV_HEADS
    for t in range(hg):
        qb = q_ref[:, (t // 2) * LANES:(t // 2 + 1) * LANES]
        half = (lane < HEAD_DIM) if t % 2 == 0 else (lane >= HEAD_DIM)
        qm = jnp.where(half, qb, jnp.zeros_like(qb))
        s = lax.dot_general(qm, kc, NT, preferred_element_type=F32) * scale
        s = jnp.where(valid, s, NEG)
        e = jnp.exp(s - jnp.max(s, axis=1, keepdims=True))
        p = e / jnp.sum(e, axis=1, keepdims=True) * anyv
        psum = psum + p
        outs.append(jnp.dot(p.astype(BF16), vc, preferred_element_type=F32))
    for r in range(hg // 2):
        o_ref[:, r * LANES:(r + 1) * LANES] = jnp.where(lane < HEAD_DIM, outs[2 * r], outs[2 * r + 1]).astype(BF16)

    imp = lax.dot_general(ov_ref[...], psum, NT, preferred_element_type=F32, precision=lax.Precision.HIGHEST)
    imp = imp[:n_sel]
    j = _iota((n_sel, tq), 0)
    cur = (i * tq + _iota((n_sel, tq), 1)) // SEL_BLOCK
    forced = (j == 0) | (j == cur) | (j == cur - 1)
    val = jnp.where(forced, BIG, jnp.where(j <= cur, imp, NEG))
    rank = _rank_rows(val, n_sel)
    member = jnp.where((rank < float(min(SEL_TOPN, n_sel))) & (j <= cur), 1.0, 0.0)
    if n_sel < LANES:
        member = jnp.concatenate([member, jnp.zeros((LANES - n_sel, tq), F32)], axis=0)
    mem_ref[0, 0] = member


def _nsa_cmp(qn, kcmp, vcmp, ovt, bsz, seq, tq=512):
    n = qn.shape[0]
    nq = seq // tq
    ncb = kcmp.shape[2]
    n_sel = seq // SEL_BLOCK
    gw = NSA_Q // NSA_KV_HEADS
    kern = functools.partial(_nsa_cmp_kernel, tq=tq, n_cmp=ncb - 1, n_sel=n_sel)
    kvspec = pl.BlockSpec((1, 1, ncb, LANES), lambda b, g, i: (b, g, 0, 0))
    return pl.pallas_call(
        kern,
        grid=(bsz, NSA_KV_HEADS, nq),
        in_specs=[pl.BlockSpec((tq, gw), lambda b, g, i: (b * nq + i, g)),
                  kvspec, kvspec,
                  pl.BlockSpec((LANES, ncb), lambda b, g, i: (0, 0))],
        out_specs=[pl.BlockSpec((tq, gw), lambda b, g, i: (b * nq + i, g)),
                   pl.BlockSpec((1, 1, LANES, tq), lambda b, g, i: (b, g, 0, i))],
        out_shape=[jax.ShapeDtypeStruct((n, NSA_Q), BF16),
                   jax.ShapeDtypeStruct((bsz, NSA_KV_HEADS, LANES, seq), F32)],
        compiler_params=_cp(("parallel", "parallel", "parallel")),
        name="nsa_cmp_select",
    )(qn, kcmp, vcmp, ovt)


def _moba_kmean_kernel(k_ref, o_ref):
    nblk = o_ref.shape[1]
    k = k_ref[...].astype(F32).reshape(nblk, MOBA_BLOCK, MOBA_W)
    o_ref[0] = jnp.sum(k, axis=1) * (1.0 / MOBA_BLOCK)


def _moba_kmean(kb, bsz, seq):
    nblk = seq // MOBA_BLOCK
    return pl.pallas_call(
        _moba_kmean_kernel,
        grid=(bsz,),
        in_specs=[pl.BlockSpec((seq, MOBA_W), lambda b: (b, 0))],
        out_specs=pl.BlockSpec((1, nblk, MOBA_W), lambda b: (b, 0, 0)),
        out_shape=jax.ShapeDtypeStruct((bsz, nblk, MOBA_W), F32),
        compiler_params=_cp(("parallel",)),
        name="moba_kmean",
    )(kb)


def _moba_select_kernel(q_ref, km_ref, mem_ref, *, tq, nblk):
    i = pl.program_id(1)
    km = km_ref[0]
    if nblk < MOBA_SLOTS:
        km = jnp.concatenate([km, jnp.zeros((MOBA_SLOTS - nblk, MOBA_W), F32)], axis=0)
    kmt = jnp.concatenate([km] * MOBA_HEADS, axis=0)
    r = _iota((LANES, MOBA_W), 0)
    c = _iota((LANES, MOBA_W), 1)
    kmt = jnp.where(r // MOBA_SLOTS == c // HEAD_DIM, kmt, 0.0).astype(BF16)
    gate = lax.dot_general(kmt, q_ref[...], NT, preferred_element_type=F32)
    members = []
    for h in range(MOBA_HEADS):
        sub = gate[h * MOBA_SLOTS:(h + 1) * MOBA_SLOTS]
        j = _iota((MOBA_SLOTS, tq), 0)
        cur = (i * tq + _iota((MOBA_SLOTS, tq), 1)) // MOBA_BLOCK
        past = j < cur
        val = jnp.where(past, sub, NEG)
        rank = _rank_rows(val, min(nblk, MOBA_SLOTS))
        members.append(jnp.where(((rank < float(MOBA_TOPK)) & past) | (j == cur), 1.0, 0.0))
    mem_ref[...] = jnp.concatenate(members, axis=0)


def _moba_select(qb, kmean, bsz, seq, tq=256):
    n = qb.shape[0]
    nq = seq // tq
    nblk = seq // MOBA_BLOCK
    kern = functools.partial(_moba_select_kernel, tq=tq, nblk=nblk)
    return pl.pallas_call(
        kern,
        grid=(bsz, nq),
        in_specs=[pl.BlockSpec((tq, MOBA_W), lambda b, i: (b * nq + i, 0)),
                  pl.BlockSpec((1, nblk, MOBA_W), lambda b, i: (b, 0, 0))],
        out_specs=pl.BlockSpec((LANES, tq), lambda b, i: (0, b * nq + i)),
        out_shape=jax.ShapeDtypeStruct((LANES, n), F32),
        compiler_params=_cp(("parallel", "parallel")),
        name="moba_select",
    )(qb, kmean)


def _flash_kernel(qi_ref, kj_ref, first_ref, last_ref, edge_ref, q_ref, k_ref, vt_ref, mem_ref, o_ref,
                  m_sc, acc_sc, qm_sc, *, mode, nhq, tq, tk):
    hg = pl.program_id(1)
    p = pl.program_id(2)
    i = qi_ref[p]
    jj = kj_ref[p]

    @pl.when(first_ref[p] == 1)
    def _():
        m_sc[...] = jnp.full(m_sc.shape, NEG, F32)
        acc_sc[...] = jnp.zeros(acc_sc.shape, F32)
        lane = _iota((1, LANES), 1)
        for t in range(nhq):
            qb = q_ref[:, (t // 2) * LANES:(t // 2 + 1) * LANES].astype(F32)
            half = (lane < HEAD_DIM) if t % 2 == 0 else (lane >= HEAD_DIM)
            qm_sc[t * tq:(t + 1) * tq, :] = jnp.where(half, qb * (HEAD_DIM ** -0.5 * LOG2E), 0.0).astype(BF16)

    def member_rows(first_row, keys_per_row):
        rows = [mem_ref[pl.ds(first_row + r, 1), :] > 0.5 for r in range(tk // keys_per_row)]
        return jnp.concatenate([jnp.broadcast_to(r, (keys_per_row, tq)) for r in rows], axis=0)

    def step(positional):
        allowed = None
        if positional:
            kpos = jj * tk + _iota((tk, tq), 0)
            qpos = i * tq + _iota((tk, tq), 1)
            allowed = kpos <= qpos
            if mode == "win":
                allowed = allowed & (kpos > qpos - WINDOW)
        if mode == "sel":
            memb = member_rows(jj * (tk // SEL_BLOCK), SEL_BLOCK)
            allowed = memb if allowed is None else allowed & memb
        k = k_ref[...]
        vt1 = jnp.concatenate([vt_ref[...], jnp.ones((ONES_ROWS, tk), BF16)], axis=0)
        s_all = lax.dot_general(k, qm_sc[...], NT, preferred_element_type=F32)
        prs, alphas = [], []
        for t in range(nhq):
            s = s_all[:, t * tq:(t + 1) * tq]
            ok = allowed
            if mode == "moba":
                memb = member_rows((hg * 2 + t) * MOBA_SLOTS + jj * (tk // MOBA_BLOCK), MOBA_BLOCK)
                ok = memb if ok is None else ok & memb
            if ok is not None:
                s = jnp.where(ok, s, NEG)
            m_old = m_sc[:, t * tq:(t + 1) * tq]
            m_new = jnp.maximum(m_old, jnp.max(s, axis=0, keepdims=True))
            prs.append(jnp.exp2(s - m_new).astype(BF16))
            alphas.append(jnp.exp2(m_old - m_new))
            m_sc[:, t * tq:(t + 1) * tq] = m_new
        pv = jnp.dot(vt1, jnp.concatenate(prs, axis=1), preferred_element_type=F32)
        acc_sc[...] = jnp.concatenate(alphas, axis=1) * acc_sc[...] + pv

    @pl.when(edge_ref[p] == 1)
    def _():
        step(True)

    @pl.when(edge_ref[p] == 0)
    def _():
        step(False)

    @pl.when(last_ref[p] == 1)
    def _():
        for r in range(nhq // 2):
            a0 = acc_sc[:, 2 * r * tq:(2 * r + 1) * tq]
            a1 = acc_sc[:, (2 * r + 1) * tq:(2 * r + 2) * tq]
            o0 = a0[:HEAD_DIM] / a0[LANES:LANES + 1]
            o1 = a1[HEAD_DIM:LANES] / a1[LANES:LANES + 1]
            ot = jnp.concatenate([o0, o1], axis=0)
            o_ref[:, r * LANES:(r + 1) * LANES] = _transpose_to_rows(ot).astype(BF16)


def _pair_tables(nq, tq, tk, window):
    qi, kj, first, last, edge = [], [], [], [], []
    for i in range(nq):
        q_lo, q_hi = i * tq, i * tq + tq - 1
        lo = 0 if window is None else max(0, (q_lo - window + 1) // tk)
        hi = q_hi // tk
        for j in range(lo, hi + 1):
            k_lo, k_hi = j * tk, j * tk + tk - 1
            inside = k_hi <= q_lo and (window is None or k_lo > q_hi - window)
            qi.append(i)
            kj.append(j)
            first.append(int(j == lo))
            last.append(int(j == hi))
            edge.append(int(not inside))
    mk = lambda a: jnp.asarray(np.asarray(a, np.int32))
    return mk(qi), mk(kj), mk(first), mk(last), mk(edge)


def _flash(mode, q, k, vt, mem, bsz, seq, tq=FLASH_TQ, tk=FLASH_TK):
    assert tk % MOBA_BLOCK == 0 and tk % SEL_BLOCK == 0 and seq % tq == 0 and seq % tk == 0
    n = q.shape[0]
    nq = seq // tq
    nk = seq // tk
    if mode == "moba":
        ngrp, nhq = MOBA_HEADS // 2, 2
        mem_spec = pl.BlockSpec((LANES, tq), lambda b, g, p, qi, kj, fi, la, ed: (0, b * nq + qi[p]))
    else:
        ngrp, nhq = NSA_KV_HEADS, NSA_HEADS // NSA_KV_HEADS
        mem_spec = pl.BlockSpec((None, None, LANES, tq), lambda b, g, p, qi, kj, fi, la, ed: (b, g, 0, qi[p]))
    tables = _pair_tables(nq, tq, tk, WINDOW if mode == "win" else None)
    npairs = int(tables[0].shape[0])
    qw = nhq // 2 * LANES
    kern = functools.partial(_flash_kernel, mode=mode, nhq=nhq, tq=tq, tk=tk)
    kspec = pl.BlockSpec((tk, LANES), lambda b, g, p, qi, kj, fi, la, ed: (b * nk + kj[p], g))
    vtspec = pl.BlockSpec((LANES, tk), lambda b, g, p, qi, kj, fi, la, ed: (g, b * nk + kj[p]))
    qspec = pl.BlockSpec((tq, qw), lambda b, g, p, qi, kj, fi, la, ed: (b * nq + qi[p], g))
    grid_spec = pltpu.PrefetchScalarGridSpec(
        num_scalar_prefetch=5,
        grid=(bsz, ngrp, npairs),
        in_specs=[qspec, kspec, vtspec, mem_spec],
        out_specs=qspec,
        scratch_shapes=[pltpu.VMEM((1, nhq * tq), F32),
                        pltpu.VMEM((LANES + ONES_ROWS, nhq * tq), F32),
                        pltpu.VMEM((nhq * tq, LANES), BF16)],
    )
    return pl.pallas_call(
        kern,
        grid_spec=grid_spec,
        out_shape=jax.ShapeDtypeStruct((n, ngrp * qw), BF16),
        compiler_params=_cp(("parallel", "parallel", "arbitrary")),
        name="flash_" + mode,
    )(*tables, q, k, vt, mem)


def _layer_norm(z, g, b):
    mu = jnp.mean(z, axis=-1, keepdims=True)
    zc = z - mu
    var = jnp.mean(zc * zc, axis=-1, keepdims=True)
    return zc * lax.rsqrt(var + LN_EPS) * g + b


HIGH_HALF = 0xFFFF0000


def _unpack_words(w):
    lo = lax.bitcast_convert_type(w << 16, F32)
    hi = lax.bitcast_convert_type(w & jnp.uint32(HIGH_HALF), F32)
    return lo, hi


def _store_slabs(slab_ref, val, rows):
    half = D_MODEL // 2
    lo = lax.bitcast_convert_type(val[:, :half].astype(BF16).astype(F32), U32) >> 16
    hi = lax.bitcast_convert_type(val[:, half:].astype(BF16).astype(F32), U32) & jnp.uint32(HIGH_HALF)
    w = lo | hi
    for c in range(SLAB):
        slab_ref[pl.ds(c, rows, stride=SLAB), :] = w[:, c * LANES:(c + 1) * LANES]


def _post_kernel(ocmp_ref, osel_ref, owin_ref, ob_ref, g_ref, ma_ref, mb_ref, x_ref,
                 eg_ref, wpa_ref, wpb_ref, wout_ref, lng_ref, lnb_ref, x1_ref, x1s_ref, *, alpha, tm):
    sig = jax.nn.sigmoid(g_ref[...])
    oa = jnp.zeros((tm, NSA_Q), F32)
    for c, o_ref in enumerate((ocmp_ref, osel_ref, owin_ref)):
        gexp = jnp.dot(sig, eg_ref[c], preferred_element_type=F32, precision=lax.Precision.HIGHEST)
        oa = oa + gexp * o_ref[...].astype(F32)
    pa = jnp.dot(oa.astype(BF16), wpa_ref[...], preferred_element_type=F32)
    pb = jnp.dot(ob_ref[...], wpb_ref[...], preferred_element_type=F32)
    merged = jax.nn.sigmoid(ma_ref[...]) * pa + jax.nn.sigmoid(mb_ref[...]) * pb
    y = jnp.dot(merged.astype(BF16), wout_ref[...], preferred_element_type=F32)
    x1 = _layer_norm(alpha * x_ref[...] + y, lng_ref[...], lnb_ref[...])
    x1_ref[...] = x1
    _store_slabs(x1s_ref, x1, tm)


def _post(ocmp, osel, owin, ob, gates, ma, mb, x2, eg, wpa, wpb, wout, lng, lnb, alpha, tm=256):
    n = x2.shape[0]
    row = lambda w: pl.BlockSpec((tm, w), lambda i: (i, 0))
    full = lambda shp: pl.BlockSpec(shp, lambda i: (0,) * len(shp))
    kern = functools.partial(_post_kernel, alpha=alpha, tm=tm)
    return pl.pallas_call(
        kern,
        grid=(n // tm,),
        in_specs=[row(NSA_Q), row(NSA_Q), row(NSA_Q), row(MOBA_W), row(LANES), row(D_MODEL), row(D_MODEL),
                  row(D_MODEL), full((3, LANES, NSA_Q)), full((NSA_Q, D_MODEL)), full((MOBA_W, D_MODEL)),
                  full((D_MODEL, D_MODEL)), full((1, D_MODEL)), full((1, D_MODEL))],
        out_specs=[row(D_MODEL), pl.BlockSpec((tm * SLAB, LANES), lambda i: (i, 0))],
        out_shape=[jax.ShapeDtypeStruct((n, D_MODEL), F32), jax.ShapeDtypeStruct((n * SLAB, LANES), U32)],
        compiler_params=_cp(("parallel",)),
        name="post_attention",
    )(ocmp, osel, owin, ob, gates, ma, mb, x2, eg, wpa, wpb, wout, lng, lnb)


def _first_argmax_rows(cur, rows, n_rows):
    mx = jnp.max(cur, axis=0, keepdims=True)
    idx = jnp.min(jnp.where(cur == mx, rows, n_rows), axis=0, keepdims=True)
    return mx, idx


def _router_kernel(x_ref, rwt_ref, bias_ref, idx_ref, gate_ref, rank_ref, cnt_ref, carry_sc, *, tm):
    i = pl.program_id(0)

    @pl.when(i == 0)
    def _():
        carry_sc[...] = jnp.zeros(carry_sc.shape, F32)

    logits = lax.dot_general(rwt_ref[...], x_ref[...].astype(BF16), NT, preferred_element_type=F32)
    sc = jax.nn.sigmoid(logits)
    biased = sc + bias_ref[...]
    gsz = N_EXPERTS // N_GROUPS
    grow = _iota((gsz, tm), 0)
    gscore = []
    for g in range(N_GROUPS):
        grp = biased[g * gsz:(g + 1) * gsz]
        m1, i1 = _first_argmax_rows(grp, grow, gsz)
        m2 = jnp.max(jnp.where(grow == i1, TAKEN, grp), axis=0, keepdims=True)
        gscore.append(m1 + m2)
    gs = jnp.concatenate(gscore, axis=0)
    keep = _rank_rows(gs, N_GROUPS) < float(TOPK_GROUPS)
    cur = jnp.concatenate(
        [jnp.where(keep[g:g + 1], biased[g * gsz:(g + 1) * gsz], NEG) for g in range(N_GROUPS)], axis=0)
    rows = _iota((N_EXPERTS, tm), 0)
    sels, tops, idxs = [], [], []
    for _ in range(TOP_K):
        _, ik = _first_argmax_rows(cur, rows, N_EXPERTS)
        selk = rows == ik
        tops.append(jnp.sum(jnp.where(selk, sc, 0.0), axis=0, keepdims=True))
        cur = jnp.where(selk, TAKEN, cur)
        sels.append(selk)
        idxs.append(ik)
    denom = tops[0]
    for tk_ in tops[1:]:
        denom = denom + tk_
    selall = jnp.where(cur == TAKEN, 1.0, 0.0)
    upper = jnp.where(_iota((tm, tm), 0) < _iota((tm, tm), 1), 1.0, 0.0).astype(BF16)
    before = jnp.dot(selall.astype(BF16), upper, preferred_element_type=F32) + carry_sc[...]
    idx_ref[...] = jnp.concatenate(idxs, axis=0)
    gate_ref[...] = jnp.concatenate([tk_ / denom * ROUTED_SCALE for tk_ in tops], axis=0)
    rank_ref[...] = jnp.concatenate(
        [jnp.sum(jnp.where(s, before, 0.0), axis=0, keepdims=True) for s in sels], axis=0).astype(jnp.int32)
    carry_sc[...] = carry_sc[...] + jnp.sum(selall, axis=1, keepdims=True)
    cnt_ref[...] = jnp.broadcast_to(carry_sc[...], cnt_ref.shape)


def _router(x1, rwt, bias, tm=256):
    n = x1.shape[0]
    kspec = pl.BlockSpec((TOP_K, tm), lambda i: (0, i))
    return pl.pallas_call(
        functools.partial(_router_kernel, tm=tm),
        grid=(n // tm,),
        in_specs=[pl.BlockSpec((tm, D_MODEL), lambda i: (i, 0)),
                  pl.BlockSpec((N_EXPERTS, D_MODEL), lambda i: (0, 0)),
                  pl.BlockSpec((N_EXPERTS, 1), lambda i: (0, 0))],
        out_specs=[kspec, kspec, kspec, pl.BlockSpec((N_EXPERTS, LANES), lambda i: (0, 0))],
        out_shape=[jax.ShapeDtypeStruct((TOP_K, n), jnp.int32), jax.ShapeDtypeStruct((TOP_K, n), F32),
                   jax.ShapeDtypeStruct((TOP_K, n), jnp.int32), jax.ShapeDtypeStruct((N_EXPERTS, LANES), F32)],
        scratch_shapes=[pltpu.VMEM((N_EXPERTS, 1), F32)],
        compiler_params=_cp(("arbitrary",)),
        name="router",
    )(x1, rwt, bias)


def _dest_kernel(idx_ref, rank_ref, pstart_ref, o_ref, *, tm):
    rows = _iota((N_EXPERTS, tm), 0)
    pstart = pstart_ref[...]
    for k in range(TOP_K):
        hit = rows == idx_ref[k:k + 1, :]
        base = jnp.sum(jnp.where(hit, pstart, 0.0), axis=0, keepdims=True)
        o_ref[:, k * tm:(k + 1) * tm] = base.astype(jnp.int32) + rank_ref[k:k + 1, :]


def _dest_rows(top_idx, rank, pad_start, tm=DISPATCH_TILE):
    n = top_idx.shape[1]
    kspec = pl.BlockSpec((TOP_K, tm), lambda i: (0, i))
    out = pl.pallas_call(
        functools.partial(_dest_kernel, tm=tm),
        grid=(n // tm,),
        in_specs=[kspec, kspec, pl.BlockSpec((N_EXPERTS, 1), lambda i: (0, 0))],
        out_specs=pl.BlockSpec((None, 1, TOP_K * tm), lambda i: (i, 0, 0)),
        out_shape=jax.ShapeDtypeStruct((n // tm, 1, TOP_K * tm), jnp.int32),
        compiler_params=_cp(("parallel",)),
        name="moe_dest_rows",
    )(top_idx, rank, pad_start.astype(F32)[:, None])
    return out.reshape(n // tm, TOP_K, tm).transpose(0, 2, 1).reshape(n // tm, tm * TOP_K)


def _issue_per_token(tm, issue):
    def body(t, carry):
        for k in range(TOP_K):
            issue(t, k)
        return carry
    lax.fori_loop(0, tm, body, 0)


def _dispatch_kernel(cnt_ref, pstart_ref, nused_ref, dest_hbm, x_ref, xs_hbm, idx_sm, zero_sc, idx_sem, row_sem,
                     zero_sem, *, ntile, tm, nblk):
    i = pl.program_id(0)
    slot = i % 2
    nrow = TOP_K * tm

    def idx_copy(t, s):
        return pltpu.make_async_copy(dest_hbm.at[t], idx_sm.at[s], idx_sem.at[s])

    def zero_copy(row, rows):
        return pltpu.make_async_copy(zero_sc.at[pl.ds(0, rows * SLAB)],
                                     xs_hbm.at[pl.ds(pl.multiple_of(row * SLAB, SLAB), rows * SLAB)], zero_sem.at[0])

    def zero_fill(act):
        def per_expert(e, carry):
            cnt = cnt_ref[e]
            npad = (ROW_BLOCK - cnt % ROW_BLOCK) % ROW_BLOCK
            row = pstart_ref[e] + cnt
            piece = ROW_BLOCK // 2
            while piece >= 1:
                @pl.when((npad & piece) != 0)
                def _(row=row, piece=piece):
                    act(zero_copy(row, piece))
                row = row + (npad & piece)
                piece //= 2
            return carry
        lax.fori_loop(0, N_EXPERTS, per_expert, 0)

        def per_block(bk, carry):
            act(zero_copy(bk * ROW_BLOCK, ROW_BLOCK))
            return carry
        lax.fori_loop(nused_ref[0], nblk, per_block, 0)

    @pl.when(i == 0)
    def _():
        idx_copy(0, 0).start()
        if ntile > 1:
            idx_copy(1, 1).start()
        zero_sc[...] = jnp.zeros(zero_sc.shape, U32)
        zero_fill(lambda cp: cp.start())
        zero_fill(lambda cp: cp.wait())

    idx_copy(i, slot).wait()

    def issue(t, k):
        src = x_ref.at[pl.ds(pl.multiple_of(t * SLAB, SLAB), SLAB)]
        dst = xs_hbm.at[pl.ds(pl.multiple_of(idx_sm[slot, t * TOP_K + k] * SLAB, SLAB), SLAB)]
        pltpu.make_async_copy(src, dst, row_sem.at[0]).start()
    _issue_per_token(tm, issue)

    @pl.when(i + 2 < ntile)
    def _():
        idx_copy(i + 2, slot).start()

    for _ in range(TOP_K):
        pltpu.make_async_copy(x_ref, xs_hbm.at[pl.ds(0, tm * SLAB)], row_sem.at[0]).wait()


def _dispatch(counts, pad_start, n_used, dest_tiles, x1s, nblk, tm=DISPATCH_TILE):
    ntile = dest_tiles.shape[0]
    grid_spec = pltpu.PrefetchScalarGridSpec(
        num_scalar_prefetch=3,
        grid=(ntile,),
        in_specs=[pl.BlockSpec(memory_space=pl.ANY),
                  pl.BlockSpec((tm * SLAB, LANES), lambda i, cn, ps, nu: (i, 0))],
        out_specs=pl.BlockSpec(memory_space=pl.ANY),
        scratch_shapes=[pltpu.SMEM((2, TOP_K * tm), jnp.int32),
                        pltpu.VMEM((ROW_BLOCK * SLAB, LANES), U32),
                        pltpu.SemaphoreType.DMA((2,)),
                        pltpu.SemaphoreType.DMA((1,)),
                        pltpu.SemaphoreType.DMA((1,))],
    )
    return pl.pallas_call(
        functools.partial(_dispatch_kernel, ntile=ntile, tm=tm, nblk=nblk),
        grid_spec=grid_spec,
        out_shape=jax.ShapeDtypeStruct((nblk * ROW_BLOCK * SLAB, LANES), U32),
        compiler_params=_cp(("arbitrary",)),
        name="moe_dispatch",
    )(counts, pad_start, n_used, dest_tiles, x1s)


def _expert_kernel(be_ref, nused_ref, xs_ref, wg_ref, wu_ref, wd_ref, y_ref, xmat, wgu_sc, wd_sc):
    i = pl.program_id(0)
    rb = ROW_BLOCK

    @pl.when(i < nused_ref[0])
    def _():
        e = be_ref[i]
        prev = be_ref[jnp.maximum(i - 1, 0)]

        @pl.when((i == 0) | (e != prev))
        def _():
            wgu_sc[:, :D_EXPERT] = wg_ref[0, 0].astype(BF16)
            wgu_sc[:, D_EXPERT:] = wu_ref[0, 0].astype(BF16)
            wd_sc[...] = wd_ref[0, 0].astype(BF16)

        half = D_MODEL // 2
        for c in range(SLAB):
            lo, hi = _unpack_words(xs_ref[pl.ds(c, rb, stride=SLAB), :])
            xmat[:, c * LANES:(c + 1) * LANES] = lo.astype(BF16)
            xmat[:, half + c * LANES:half + (c + 1) * LANES] = hi.astype(BF16)
        hgu = jnp.dot(xmat[...], wgu_sc[...], preferred_element_type=F32)
        h = jax.nn.silu(hgu[:, :D_EXPERT]) * hgu[:, D_EXPERT:]
        y = jnp.dot(h.astype(BF16), wd_sc[...], preferred_element_type=F32)
        _store_slabs(y_ref, y, rb)

    @pl.when(i >= nused_ref[0])
    def _():
        y_ref[...] = jnp.zeros(y_ref.shape, U32)


def _experts(blk_expert, n_used, xs, w_gate, w_up, w_down, layer):
    nblk = blk_expert.shape[0]
    rb = ROW_BLOCK
    blk = lambda i, nu: jnp.minimum(i, nu[0] - 1)
    wspec = lambda shp: pl.BlockSpec((1, 1) + shp, lambda i, be, nu: (layer, be[blk(i, nu)], 0, 0))
    grid_spec = pltpu.PrefetchScalarGridSpec(
        num_scalar_prefetch=2,
        grid=(nblk,),
        in_specs=[pl.BlockSpec((rb * SLAB, LANES), lambda i, be, nu: (blk(i, nu), 0)),
                  wspec((D_MODEL, D_EXPERT)), wspec((D_MODEL, D_EXPERT)), wspec((D_EXPERT, D_MODEL))],
        out_specs=pl.BlockSpec((rb * SLAB, LANES), lambda i, be, nu: (i, 0)),
        scratch_shapes=[pltpu.VMEM((rb, D_MODEL), BF16),
                        pltpu.VMEM((D_MODEL, 2 * D_EXPERT), BF16),
                        pltpu.VMEM((D_EXPERT, D_MODEL), BF16)],
    )
    return pl.pallas_call(
        _expert_kernel,
        grid_spec=grid_spec,
        out_shape=jax.ShapeDtypeStruct((nblk * rb * SLAB, LANES), U32),
        compiler_params=_cp(("arbitrary",)),
        name="moe_experts",
    )(blk_expert, n_used, xs, w_gate, w_up, w_down)


def _combine_kernel(dest_hbm, y_hbm, gate_ref, x1_ref, wsg_ref, wsu_ref, wsd_ref, lng_ref, lnb_ref, o_ref,
                    idx_sm, ybuf, gbuf, acc_sc, idx_sem, row_sem, *, ntile, tm, alpha):
    i = pl.program_id(0)
    slot = i % 2
    nrow = TOP_K * tm

    def idx_copy(t, s):
        return pltpu.make_async_copy(dest_hbm.at[t], idx_sm.at[s], idx_sem.at[s])

    def gather(s):
        def issue(t, k):
            src = y_hbm.at[pl.ds(pl.multiple_of(idx_sm[s, t * TOP_K + k] * SLAB, SLAB), SLAB)]
            dst = ybuf.at[s, pl.ds(pl.multiple_of((k * tm + t) * SLAB, SLAB), SLAB)]
            pltpu.make_async_copy(src, dst, row_sem.at[s]).start()
        _issue_per_token(tm, issue)

    @pl.when(i == 0)
    def _():
        idx_copy(0, 0).start()
        idx_copy(0, 0).wait()
        gather(0)
        if ntile > 1:
            idx_copy(1, 1).start()

    @pl.when(i + 1 < ntile)
    def _():
        idx_copy(i + 1, 1 - slot).wait()
        gather(1 - slot)

    @pl.when(i + 2 < ntile)
    def _():
        idx_copy(i + 2, slot).start()

    x1 = x1_ref[...]
    xb = x1.astype(BF16)
    hs = jax.nn.silu(jnp.dot(xb, wsg_ref[...], preferred_element_type=F32)) * jnp.dot(
        xb, wsu_ref[...], preferred_element_type=F32)
    shared = jnp.dot(hs.astype(BF16), wsd_ref[...], preferred_element_type=F32)

    for k in range(TOP_K):
        gcol = _transpose_to_rows(jnp.broadcast_to(gate_ref[k:k + 1, :], (LANES, tm)))
        for c in range(SLAB):
            gbuf[pl.ds(k * tm * SLAB + c, tm, stride=SLAB), :] = gcol

    pltpu.make_async_copy(y_hbm.at[pl.ds(0, nrow * SLAB)], ybuf.at[slot], row_sem.at[slot]).wait()
    acc_lo = jnp.zeros((tm * SLAB, LANES), F32)
    acc_hi = jnp.zeros((tm * SLAB, LANES), F32)
    for k in range(TOP_K):
        lo, hi = _unpack_words(ybuf[slot, pl.ds(k * tm * SLAB, tm * SLAB), :])
        g = gbuf[pl.ds(k * tm * SLAB, tm * SLAB), :]
        acc_lo = acc_lo + lo * g
        acc_hi = acc_hi + hi * g
    acc_sc[0] = acc_lo
    acc_sc[1] = acc_hi
    routed = jnp.concatenate([acc_sc[h, pl.ds(c, tm, stride=SLAB), :] for h in range(2) for c in range(SLAB)],
                             axis=1)
    o_ref[...] = _layer_norm(alpha * x1 + routed + shared, lng_ref[...], lnb_ref[...])


def _combine(dest_tiles, y_slab, gate, x1, wsg, wsu, wsd, lng, lnb, alpha, tm=DISPATCH_TILE):
    n = x1.shape[0]
    ntile = n // tm
    nrow = TOP_K * tm
    full = lambda shp: pl.BlockSpec(shp, lambda i: (0,) * len(shp))
    kern = functools.partial(_combine_kernel, ntile=ntile, tm=tm, alpha=alpha)
    return pl.pallas_call(
        kern,
        grid=(ntile,),
        in_specs=[pl.BlockSpec(memory_space=pl.ANY),
                  pl.BlockSpec(memory_space=pl.ANY),
                  pl.BlockSpec((TOP_K, tm), lambda i: (0, i)),
                  pl.BlockSpec((tm, D_MODEL), lambda i: (i, 0)),
                  full((D_MODEL, D_EXPERT)), full((D_MODEL, D_EXPERT)), full((D_EXPERT, D_MODEL)),
                  full((1, D_MODEL)), full((1, D_MODEL))],
        out_specs=pl.BlockSpec((tm, D_MODEL), lambda i: (i, 0)),
        out_shape=jax.ShapeDtypeStruct((n, D_MODEL), F32),
        scratch_shapes=[pltpu.SMEM((2, nrow), jnp.int32),
                        pltpu.VMEM((2, nrow * SLAB, LANES), U32),
                        pltpu.VMEM((nrow * SLAB, LANES), F32),
                        pltpu.VMEM((2, tm * SLAB, LANES), F32),
                        pltpu.SemaphoreType.DMA((2,)),
                        pltpu.SemaphoreType.DMA((2,))],
        compiler_params=_cp(("arbitrary",)),
        name="moe_combine",
    )(dest_tiles, y_slab, gate, x1, wsg, wsu, wsd, lng, lnb)


def _rope_tables(seq):
    pos = jnp.arange(seq, dtype=F32)
    inv = ROPE_THETA ** (-jnp.arange(0, HEAD_DIM, 2, dtype=F32) / HEAD_DIM)
    ang = pos[:, None] * inv[None, :]
    ang = jnp.concatenate([ang] * (LANES // (HEAD_DIM // 2)), axis=-1)
    first = (np.arange(LANES) % HEAD_DIM) < HEAD_DIM // 2
    sin = jnp.sin(ang)
    return jnp.cos(ang), jnp.where(first, -sin, 0.0), jnp.where(first, 0.0, sin)


def _pack_inproj(w_in, b_in):
    cols = _INPROJ_COLS
    safe = np.where(cols < 0, 0, cols)
    live = jnp.asarray(cols >= 0)
    w = jnp.where(live[None, :], w_in[:, safe], 0.0).astype(BF16)
    b = jnp.where(live, b_in[safe], 0.0)[None, :]
    wt = w_in[:, _INPROJ_TCOLS].T.astype(BF16)
    bt = b_in[_INPROJ_TCOLS][:, None]
    return w, b, wt, bt


def _pack_compress(cmp_pos, cmp_w1, cmp_b1, cmp_w2):
    half = CMP_BLOCK // 2
    g = NSA_KV_HEADS

    def w_half(w1, lo):
        w = w1.reshape(2, CMP_BLOCK, HEAD_DIM, CMP_HIDDEN)[:, lo:lo + half]
        eye = jnp.eye(g, dtype=F32)
        w = jnp.einsum("klde,gh->klgdhe", w, eye)
        return w.reshape(2, half * g * HEAD_DIM, g * CMP_HIDDEN).astype(BF16)

    def p_half(lo):
        p = cmp_pos[:, lo:lo + half]
        p = jnp.broadcast_to(p[:, :, None, :], (2, half, g, HEAD_DIM))
        return p.reshape(2, 1, half * g * HEAD_DIM)

    b1 = jnp.concatenate([cmp_b1] * g, axis=-1)[:, None, :]
    w2 = jnp.concatenate([cmp_w2, cmp_w2], axis=-1).astype(BF16)
    return p_half(0), p_half(half), w_half(cmp_w1, 0), w_half(cmp_w1, half), b1, w2


def _overlap_t(seq):
    ncb = seq // CMP_STRIDE
    n_cmp = ncb - 1
    n_sel = seq // SEL_BLOCK
    c0 = np.arange(ncb)[None, :] * CMP_STRIDE
    s0 = np.arange(LANES)[:, None] * SEL_BLOCK
    ov = np.clip(np.minimum(c0 + CMP_BLOCK, s0 + SEL_BLOCK) - np.maximum(c0, s0), 0, None) / CMP_STRIDE
    ov = ov * (np.arange(ncb)[None, :] < n_cmp) * (np.arange(LANES)[:, None] < n_sel)
    return jnp.asarray(ov, F32)


def _gate_expand():
    r = np.arange(LANES)[None, :, None]
    col = np.arange(NSA_Q)[None, None, :]
    c = np.arange(3)[:, None, None]
    return jnp.asarray(r == (col // HEAD_DIM) * 3 + c, F32)


def _block_plan(counts, n_tok):
    rb = ROW_BLOCK
    nblk = -(-(n_tok * TOP_K + N_EXPERTS * (rb - 1)) // rb)
    padded = (counts + rb - 1) // rb * rb
    pad_end = jnp.cumsum(padded)
    pad_start = (pad_end - padded).astype(jnp.int32)
    first_row = jnp.arange(nblk, dtype=jnp.int32) * rb
    blk_expert = jnp.sum((pad_end[None, :] <= first_row[:, None]).astype(jnp.int32), axis=1)
    blk_expert = jnp.minimum(blk_expert, N_EXPERTS - 1).astype(jnp.int32)
    n_used = (pad_end[-1:] // rb).astype(jnp.int32)
    return pad_start, blk_expert, n_used, nblk


def kernel(x, w_in, b_in, cmp_pos, cmp_w1, cmp_b1, cmp_w2, w_proj_a, w_proj_b, w_out, ln1_g, ln1_b,
           router_w, router_bias, w_gate, w_up, w_down, ws_gate, ws_up, ws_down, ln2_g, ln2_b):
    bsz, seq, d = x.shape
    depth = w_in.shape[0]
    n = bsz * seq
    alpha = float((2 * depth) ** 0.25)
    cos, sa, sb = _rope_tables(seq)
    ovt = _overlap_t(seq)
    eg = _gate_expand()
    ncb = seq // CMP_STRIDE
    x2 = x.reshape(n, d)
    for l in range(depth):
        (qn, qnr, kc, vc, ks, kw, gates, qb, kb, ma, mb, vst, vwt, vbt) = _inproj(
            x2, *_pack_inproj(w_in[l], b_in[l]), cos, sa, sb, seq)
        cw = CMP_STRIDE * NSA_KV
        kcmp, vcmp = _compress(kc.reshape(bsz, ncb, cw), vc.reshape(bsz, ncb, cw),
                               *_pack_compress(cmp_pos[l], cmp_w1[l], cmp_b1[l], cmp_w2[l]))
        ocmp, member = _nsa_cmp(qn, kcmp, vcmp, ovt, bsz, seq)
        osel = _flash("sel", qnr, ks, vst, member, bsz, seq)
        owin = _flash("win", qnr, kw, vwt, member, bsz, seq)
        mmem = _moba_select(qb, _moba_kmean(kb, bsz, seq), bsz, seq)
        ob = _flash("moba", qb, kb, vbt, mmem, bsz, seq, tq=MOBA_TQ, tk=MOBA_TQ)
        x1, x1s = _post(ocmp, osel, owin, ob, gates, ma, mb, x2, eg,
                        w_proj_a[l].astype(BF16), w_proj_b[l].astype(BF16), w_out[l].astype(BF16),
                        ln1_g[l][None, :], ln1_b[l][None, :], alpha)
        top_idx, gate, rank, cnt = _router(x1, router_w[l].T.astype(BF16), router_bias[l][:, None])
        counts = cnt[:, 0].astype(jnp.int32)
        pad_start, blk_expert, n_used, nblk = _block_plan(counts, n)
        dest_tiles = _dest_rows(top_idx, rank, pad_start)
        xs = _dispatch(counts, pad_start, n_used, dest_tiles, x1s, nblk)
        y_slab = _experts(blk_expert, n_used, xs, w_gate, w_up, w_down, l)
        x2 = _combine(dest_tiles, y_slab, gate, x1, ws_gate[l].astype(BF16), ws_up[l].astype(BF16),
                      ws_down[l].astype(BF16), ln2_g[l][None, :], ln2_b[l][None, :], alpha)
    return x2.reshape(bsz, seq, d)
```
